```python
import math
import jax, jax.numpy as jnp
from jax import lax
import numpy as np

D_MODEL = 1024
BATCH = 32
SEQ = 2048
DEPTH = 4

N_MIXERS = 2
RWKV_HEAD = 64
RWKV_HEADS = D_MODEL // RWKV_HEAD
DECAY_LORA = 64
AAA_LORA = 64
MV_LORA = 32
GATE_LORA = 160
GN_EPS = 64e-5
N_RWKV_BRANCHES = 6
ATT_HEADS = 16
QK_HEAD = 64
V_HEAD = 64
Q_LORA = 256
KV_LORA = 128
IDX_HEADS = 8
IDX_DIM = 64
TOPK_MAX = 256
Q_BLOCK = 128
DSA_IN = Q_LORA + KV_LORA + IDX_DIM + IDX_HEADS
REL_BUCKETS = 32
REL_MAX_DIST = 128
D_FF = 2816
CONV_W = 3
DN_ALPHA = (2 * DEPTH) ** 0.25
DN_BETA = (8 * DEPTH) ** -0.25
LN_EPS = 1e-5

kernel_name = "hybrid_rwkv7_dsa_convffn_deepnorm"


def layer_norm(x, g, b, eps=LN_EPS):
    xf = x.astype(jnp.float32)
    mu = xf.mean(-1, keepdims=True)
    var = jnp.square(xf - mu).mean(-1, keepdims=True)
    return ((xf - mu) * lax.rsqrt(var + eps) * g + b).astype(x.dtype)


def rms_norm(x, g, eps=1e-6):
    xf = x.astype(jnp.float32)
    return (xf * lax.rsqrt(jnp.square(xf).mean(-1, keepdims=True) + eps) * g).astype(x.dtype)


def token_shift(x):
    return jnp.pad(x, ((0, 0), (1, 0), (0, 0)))[:, :-1]


def t5_bucket(dist):
    n = jnp.maximum(dist, 0)
    max_exact = REL_BUCKETS // 2
    nf = jnp.maximum(n, 1).astype(jnp.float32)
    large = max_exact + (jnp.log(nf / max_exact) / math.log(REL_MAX_DIST / max_exact)
                         * (REL_BUCKETS - max_exact)).astype(jnp.int32)
    large = jnp.minimum(large, REL_BUCKETS - 1)
    return jnp.where(n < max_exact, n, large)


def _wkv7_step(state, inp):
    r_t, w_t, k_t, v_t, a_t, b_t = inp
    sa = jnp.einsum('bhvk,bhk->bhv', state, a_t)
    state = (state * w_t[:, :, None, :] + sa[..., None] * b_t[:, :, None, :]
             + v_t[..., None] * k_t[:, :, None, :])
    y = jnp.einsum('bhvk,bhk->bhv', state, r_t)
    return state, y


def rwkv7_time_mix(x, v_first, vres, mix, w_rkv, w0, w1, w2, a0, a1, a2, g1, g2,
                   k_k, k_a, r_k, lnx_g, lnx_b, w_o):
    B, S, D = x.shape
    H, N = RWKV_HEADS, RWKV_HEAD
    f32 = jnp.float32
    xx = token_shift(x) - x
    xm = x[None] + xx[None] * mix[:, None, None, :]
    r, k, v = jnp.einsum('nbsd,nde->nbse', xm[:3], w_rkv)
    xv, xw, xa, xg = xm[2], xm[3], xm[4], xm[5]
    w_log = -jax.nn.softplus(-(w0 + jnp.tanh(xw @ w1) @ w2)) - 0.5
    decay = jnp.exp(-jnp.exp(w_log.astype(f32)))
    if vres is None:
        v_first = v
    else:
        v0, v1, v2 = vres
        v = v + (v_first - v) * jax.nn.sigmoid(v0 + (xv @ v1) @ v2)
    a = jax.nn.sigmoid(a0 + (xa @ a1) @ a2)
    g = jax.nn.sigmoid(xg @ g1) @ g2
    heads = lambda t: t.reshape(B, S, H, N)
    kk = heads(k * k_k).astype(f32)
    kk = kk / jnp.maximum(jnp.sqrt(jnp.sum(kk * kk, -1, keepdims=True)), 1e-12)
    k = k * (1 + (a - 1) * k_a)
    rh, kh, vh, ah = heads(r), heads(k), heads(v), heads(a)
    tm = lambda t: jnp.moveaxis(t.astype(f32), 1, 0)
    xs = (tm(rh), tm(heads(decay)), tm(kh), tm(vh), tm(-kk), tm(kk * ah.astype(f32)))
    state0 = jnp.zeros((B, H, N, N), f32)
    _, y = lax.scan(_wkv7_step, state0, xs)
    y = jnp.moveaxis(y, 0, 1)
    mu = y.mean(-1, keepdims=True)
    var = jnp.square(y - mu).mean(-1, keepdims=True)
    y = ((y - mu) * lax.rsqrt(var + GN_EPS)).reshape(B, S, D) * lnx_g + lnx_b
    bonus = jnp.sum(rh * kh * r_k, -1, keepdims=True) * vh
    y = (y.astype(x.dtype) + bonus.reshape(B, S, D)) * g
    return y @ w_o, v_first


def dsa_attention(x, w_in, q_norm_g, kv_norm_g, w_uq, w_uk, w_uv, w_qidx, kidx_g, kidx_b,
                  rel_bias, w_o):
    B, S, D = x.shape
    H = ATT_HEADS
    topk = min(TOPK_MAX, S // 4)
    h = x @ w_in
    c_q, c_kv, k_idx, w_idx = jnp.split(
        h, [Q_LORA, Q_LORA + KV_LORA, Q_LORA + KV_LORA + IDX_DIM], axis=-1)
    c_q = rms_norm(c_q, q_norm_g)
    c_kv = rms_norm(c_kv, kv_norm_g)
    q = (c_q @ w_uq).reshape(B, S, H, QK_HEAD)
    q_idx = (c_q @ w_qidx).reshape(B, S, IDX_HEADS, IDX_DIM)
    k_idx = layer_norm(k_idx, kidx_g, kidx_b)
    w_idx = w_idx * (IDX_HEADS ** -0.5 * IDX_DIM ** -0.5)
    nb = S // Q_BLOCK
    blk = lambda t: jnp.swapaxes(t.reshape((B, nb, Q_BLOCK) + t.shape[2:]), 0, 1)
    starts = jnp.arange(nb, dtype=jnp.int32) * Q_BLOCK
    key_pos = jnp.arange(S, dtype=jnp.int32)
    qk_scale = QK_HEAD ** -0.5

    def block(args):
        q_b, qi_b, wi_b, t0 = args
        t_pos = t0 + jnp.arange(Q_BLOCK, dtype=jnp.int32)
        s_idx = jnp.einsum('bthd,bsd->bths', qi_b, k_idx)
        score = jnp.einsum('bths,bth->bts', jax.nn.relu(s_idx), wi_b).astype(jnp.float32)
        causal = key_pos[None, :] <= t_pos[:, None]
        score = jnp.where(causal[None], score, -jnp.inf)
        _, sel = lax.top_k(score, topk)
        kv_sel = jax.vmap(lambda c, i: c[i])(c_kv, sel)
        q_abs = jnp.einsum('bthd,hdc->bthc', q_b, w_uk)
        logits = jnp.einsum('bthc,btkc->bhtk', q_abs, kv_sel).astype(jnp.float32) * qk_scale
        dist = t_pos[None, :, None] - sel
        bias = rel_bias[t5_bucket(dist)].astype(jnp.float32)
        logits = logits + jnp.moveaxis(bias, -1, 1)
        logits = jnp.where((dist >= 0)[:, None], logits, -jnp.inf)
        p = jax.nn.softmax(logits, axis=-1).astype(x.dtype)
        o_lat = jnp.einsum('bhtk,btkc->bthc', p, kv_sel)
        o = jnp.einsum('bthc,hcv->bthv', o_lat, w_uv)
        return o.reshape(B, Q_BLOCK, H * V_HEAD)

    o = lax.map(block, (blk(q), blk(q_idx), blk(w_idx), starts))
    o = jnp.swapaxes(o, 0, 1).reshape(B, S, H * V_HEAD)
    return o @ w_o


def conv_ffn(x, w_up, conv_w, conv_b, w_down):
    S = x.shape[1]
    u = x @ w_up
    up = jnp.pad(u, ((0, 0), (CONV_W - 1, 0), (0, 0)))
    u = sum(up[:, j:j + S] * conv_w[j] for j in range(CONV_W)) + conv_b
    gate, val = jnp.split(u, 2, axis=-1)
    return (jax.nn.silu(gate) * val) @ w_down


def setup_inputs(seed: int = 0) -> dict:
    key = jax.random.key(seed)
    ks = iter(jax.random.split(key, 64))
    f32 = jnp.float32
    n = lambda shape, scale: scale * jax.random.normal(next(ks), shape, f32)
    D, H, N = D_MODEL, RWKV_HEADS, RWKV_HEAD
    n_rwkv = (DEPTH + 1) // 2
    n_dsa = DEPTH // 2
    n_vres = max(n_rwkv - 1, 0)
    return {
        "x": n((BATCH, SEQ, D), 1.0),
        "ln_g": 1.0 + n((DEPTH, 2, D), 0.02),
        "ln_b": n((DEPTH, 2, D), 0.02),
        "rwkv_mix": jax.random.uniform(next(ks), (n_rwkv, N_RWKV_BRANCHES, D), f32),
        "rwkv_w_rkv": n((n_rwkv, 3, D, D), D ** -0.5),
        "rwkv_w0": -1.5 + n((n_rwkv, D), 1.0),
        "rwkv_w1": n((n_rwkv, D, DECAY_LORA), D ** -0.5),
        "rwkv_w2": n((n_rwkv, DECAY_LORA, D), 0.5 * DECAY_LORA ** -0.5),
        "rwkv_a0": n((n_rwkv, D), 0.1),
        "rwkv_a1": n((n_rwkv, D, AAA_LORA), D ** -0.5),
        "rwkv_a2": n((n_rwkv, AAA_LORA, D), 0.5 * AAA_LORA ** -0.5),
        "rwkv_v0": n((n_vres, D), 0.1),
        "rwkv_v1": n((n_vres, D, MV_LORA), D ** -0.5),
        "rwkv_v2": n((n_vres, MV_LORA, D), 0.5 * MV_LORA ** -0.5),
        "rwkv_g1": n((n_rwkv, D, GATE_LORA), D ** -0.5),
        "rwkv_g2": n((n_rwkv, GATE_LORA, D), GATE_LORA ** -0.5),
        "rwkv_k_k": 0.85 + n((n_rwkv, D), 0.05),
        "rwkv_k_a": 1.0 + n((n_rwkv, D), 0.05),
        "rwkv_r_k": n((n_rwkv, H, N), 0.1),
        "rwkv_lnx_g": 1.0 + n((n_rwkv, D), 0.02),
        "rwkv_lnx_b": n((n_rwkv, D), 0.02),
        "rwkv_w_o": n((n_rwkv, D, D), DN_BETA * D ** -0.5),
        "dsa_w_in": n((n_dsa, D, DSA_IN), D ** -0.5),
        "dsa_q_norm_g": 1.0 + n((n_dsa, Q_LORA), 0.02),
        "dsa_kv_norm_g": 1.0 + n((n_dsa, KV_LORA), 0.02),
        "dsa_w_uq": n((n_dsa, Q_LORA, ATT_HEADS * QK_HEAD), Q_LORA ** -0.5),
        "dsa_w_uk": n((n_dsa, ATT_HEADS, QK_HEAD, KV_LORA), KV_LORA ** -0.5),
        "dsa_w_uv": n((n_dsa, ATT_HEADS, KV_LORA, V_HEAD), KV_LORA ** -0.5),
        "dsa_w_qidx": n((n_dsa, Q_LORA, IDX_HEADS * IDX_DIM), Q_LORA ** -0.5),
        "dsa_kidx_g": 1.0 + n((n_dsa, IDX_DIM), 0.02),
        "dsa_kidx_b": n((n_dsa, IDX_DIM), 0.02),
        "dsa_w_o": n((n_dsa, ATT_HEADS * V_HEAD, D), DN_BETA * (ATT_HEADS * V_HEAD) ** -0.5),
        "rel_bias": n((REL_BUCKETS, ATT_HEADS), 0.5),
        "ffn_w_up": n((DEPTH, D, 2 * D_FF), D ** -0.5),
        "ffn_conv_w": n((DEPTH, CONV_W, 2 * D_FF), CONV_W ** -0.5),
        "ffn_conv_b": n((DEPTH, 2 * D_FF), 0.02),
        "ffn_w_down": n((DEPTH, D_FF, D), DN_BETA * D_FF ** -0.5),
    }


def reference(x, ln_g, ln_b, rwkv_mix, rwkv_w_rkv, rwkv_w0, rwkv_w1, rwkv_w2, rwkv_a0, rwkv_a1,
              rwkv_a2, rwkv_v0, rwkv_v1, rwkv_v2, rwkv_g1, rwkv_g2, rwkv_k_k, rwkv_k_a, rwkv_r_k,
              rwkv_lnx_g, rwkv_lnx_b, rwkv_w_o, dsa_w_in, dsa_q_norm_g, dsa_kv_norm_g, dsa_w_uq,
              dsa_w_uk, dsa_w_uv, dsa_w_qidx, dsa_kidx_g, dsa_kidx_b, dsa_w_o, rel_bias,
              ffn_w_up, ffn_conv_w, ffn_conv_b, ffn_w_down):
    v_first = None
    for i in range(DEPTH):
        j = i // N_MIXERS
        if i % N_MIXERS == 0:
            vres = None if j == 0 else (rwkv_v0[j - 1], rwkv_v1[j - 1], rwkv_v2[j - 1])
            h, v_first = rwkv7_time_mix(
                x, v_first, vres, rwkv_mix[j], rwkv_w_rkv[j], rwkv_w0[j], rwkv_w1[j], rwkv_w2[j],
                rwkv_a0[j], rwkv_a1[j], rwkv_a2[j], rwkv_g1[j], rwkv_g2[j], rwkv_k_k[j],
                rwkv_k_a[j], rwkv_r_k[j], rwkv_lnx_g[j], rwkv_lnx_b[j], rwkv_w_o[j])
        else:
            h = dsa_attention(
                x, dsa_w_in[j], dsa_q_norm_g[j], dsa_kv_norm_g[j], dsa_w_uq[j], dsa_w_uk[j],
                dsa_w_uv[j], dsa_w_qidx[j], dsa_kidx_g[j], dsa_kidx_b[j], rel_bias, dsa_w_o[j])
        x = layer_norm(DN_ALPHA * x + h, ln_g[i, 0], ln_b[i, 0])
        f = conv_ffn(x, ffn_w_up[i], ffn_conv_w[i], ffn_conv_b[i], ffn_w_down[i])
        x = layer_norm(DN_ALPHA * x + f, ln_g[i, 1], ln_b[i, 1])
    return x
```

```python
import functools
import math

import jax
import jax.numpy as jnp
from jax import lax
from jax.experimental import pallas as pl
from jax.experimental.pallas import tpu as pltpu

F32 = jnp.float32
BF16 = jnp.bfloat16
HIGHEST = lax.Precision.HIGHEST

N_HEAD = 64
ATT_HEADS = 16
QK_HEAD = 64
V_HEAD = 64
Q_LORA = 256
KV_LORA = 128
IDX_HEADS = 8
IDX_DIM = 64
TOPK_MAX = 256
Q_BLOCK = 128
REL_BUCKETS = 32
REL_MAX_DIST = 128
CONV_W = 3
GN_EPS = 64e-5
LN_EPS = 1e-5

LANES = 128
SUBLANES = 8
VMEM_LIMIT = 56 * 1024 * 1024

NEG_MASK = -1e30
INT_MIN = -2 ** 31


def _bdot(a, b):
    return jnp.dot(a.astype(BF16), b.astype(BF16), preferred_element_type=F32)


def _bdot_nt(a, b):
    return lax.dot_general(a.astype(BF16), b.astype(BF16), (((1,), (1,)), ((), ())),
                           preferred_element_type=F32)


def _hdot(a, b):
    return jnp.dot(a, b, precision=HIGHEST, preferred_element_type=F32)


def _layer_norm(x, g, b):
    mu = jnp.mean(x, axis=-1, keepdims=True)
    xc = x - mu
    var = jnp.mean(xc * xc, axis=-1, keepdims=True)
    return xc * lax.rsqrt(var + LN_EPS) * g + b


def _head_sum(x, p, pt):
    return _hdot(_hdot(x, p), pt)


def _const_spec(shape):
    nd = len(shape)
    return pl.BlockSpec(shape, lambda *_: (0,) * nd)


def _prev_rows_spec(ts, d):
    return pl.BlockSpec((SUBLANES, d), lambda i: (jnp.maximum(i * (ts // SUBLANES) - 1, 0), 0))


def _rwkv_proj_kernel(*refs, ts, seq, has_vres):
    (xp_ref, x_ref, mix_ref, wr_ref, wk_ref, wv_ref, w0_ref, w1_ref, w2_ref, a0_ref, a1_ref,
     a2_ref, g1_ref, g2_ref, kk_ref, ka_ref, p_ref, pt_ref) = refs[:18]
    if has_vres:
        vf_ref, v0_ref, v1_ref, v2_ref = refs[18:22]
        outs = refs[22:]
    else:
        outs = refs[18:]
    r_o, w_o, k_o, v_o, a_o, b_o, g_o = outs

    first = (pl.program_id(0) * ts) % seq == 0
    x = x_ref[...]
    p_last = jnp.where(first, 0.0, xp_ref[SUBLANES - 1:SUBLANES, :])
    rows = lax.broadcasted_iota(jnp.int32, (ts, 1), 0)
    x_shift = jnp.where(rows == 0, p_last, pltpu.roll(x, 1, 0))
    xx = x_shift - x
    mix = mix_ref[...]

    def xm(j):
        return x + xx * mix[j:j + 1]

    r = _bdot(xm(0), wr_ref[...])
    k = _bdot(xm(1), wk_ref[...])
    xv = xm(2)
    v = _bdot(xv, wv_ref[...])
    w_log = -jax.nn.softplus(-(w0_ref[...] + _bdot(jnp.tanh(_bdot(xm(3), w1_ref[...])), w2_ref[...]))) - 0.5
    decay = jnp.exp(-jnp.exp(w_log))
    if has_vres:
        v = v + (vf_ref[...] - v) * jax.nn.sigmoid(v0_ref[...] + _bdot(_bdot(xv, v1_ref[...]), v2_ref[...]))
    a = jax.nn.sigmoid(a0_ref[...] + _bdot(_bdot(xm(4), a1_ref[...]), a2_ref[...]))
    g = _bdot(jax.nn.sigmoid(_bdot(xm(5), g1_ref[...])), g2_ref[...])
    kk = k * kk_ref[...]
    ss = _head_sum(kk * kk, p_ref[...], pt_ref[...])
    kk = kk / jnp.maximum(jnp.sqrt(ss), 1e-12)
    k = k * (1.0 + (a - 1.0) * ka_ref[...])
    r_o[...] = r
    w_o[...] = decay
    k_o[...] = k
    v_o[...] = v
    a_o[...] = -kk
    b_o[...] = kk * a
    g_o[...] = g


def _rwkv_proj(x2d, seq, mix, w_rkv, w0, w1, w2, a0, a1, a2, g1, g2, k_k, k_a, p, pt, vres, ts=256):
    t, d = x2d.shape
    row = lambda v: v.reshape(1, d)
    tok = pl.BlockSpec((ts, d), lambda i: (i, 0))
    ins = [x2d, x2d, mix, w_rkv[0].astype(BF16), w_rkv[1].astype(BF16), w_rkv[2].astype(BF16),
           row(w0), w1.astype(BF16), w2.astype(BF16), row(a0), a1.astype(BF16), a2.astype(BF16),
           g1.astype(BF16), g2.astype(BF16), row(k_k), row(k_a), p, pt]
    specs = [_prev_rows_spec(ts, d), tok] + [_const_spec(a.shape) for a in ins[2:]]
    if vres is not None:
        v_first, v0, v1, v2 = vres
        extra = [v_first, row(v0), v1.astype(BF16), v2.astype(BF16)]
        ins += extra
        specs += [tok] + [_const_spec(a.shape) for a in extra[1:]]
    out_shape = [jax.ShapeDtypeStruct((t, d), F32)] * 7
    return pl.pallas_call(
        functools.partial(_rwkv_proj_kernel, ts=ts, seq=seq, has_vres=vres is not None),
        grid=(t // ts,),
        in_specs=specs,
        out_specs=[tok] * 7,
        out_shape=out_shape,
        compiler_params=pltpu.CompilerParams(dimension_semantics=("parallel",),
                                             vmem_limit_bytes=VMEM_LIMIT),
        name="rwkv_proj",
    )(*ins)


def _wkv_kernel(r_ref, w_ref, k_ref, v_ref, a_ref, b_ref, y_ref, s_scr, *, tc, n):
    @pl.when(pl.program_id(1) == 0)
    def _():
        s_scr[...] = jnp.zeros_like(s_scr)

    def step(t, carry):
        a = a_ref[t]
        w = w_ref[t]
        b = b_ref[t]
        k = k_ref[t]
        r = r_ref[t]

        def vloop(vi, c):
            s = s_scr[vi]
            sa = jnp.sum(s * a, axis=0, keepdims=True)
            vv = v_ref[t, pl.ds(vi, 1), :]
            s_new = s * w + sa * b + vv * k
            s_scr[vi] = s_new
            y_ref[t, pl.ds(vi, 1), :] = jnp.sum(s_new * r, axis=0, keepdims=True)
            return c

        return lax.fori_loop(0, n, vloop, carry, unroll=4)

    lax.fori_loop(0, tc, step, 0)


def _wkv(r, w, k, v, a, b, tc=16):
    s, n, pairs = r.shape
    spec = pl.BlockSpec((tc, n, LANES), lambda p, c: (c, 0, p))
    return pl.pallas_call(
        functools.partial(_wkv_kernel, tc=tc, n=n),
        grid=(pairs // LANES, s // tc),
        in_specs=[spec] * 6,
        out_specs=spec,
        out_shape=jax.ShapeDtypeStruct((s, n, pairs), F32),
        scratch_shapes=[pltpu.VMEM((n, n, LANES), F32)],
        compiler_params=pltpu.CompilerParams(dimension_semantics=("parallel", "arbitrary"),
                                             vmem_limit_bytes=VMEM_LIMIT),
        name="wkv7_scan",
    )(r, w, k, v, a, b)


def _rwkv_out_kernel(y_ref, r_ref, k_ref, v_ref, g_ref, x_ref, rk_ref, lxg_ref, lxb_ref, wo_ref,
                     lng_ref, lnb_ref, p_ref, pt_ref, o_ref, *, alpha):
    p = p_ref[...]
    pt = pt_ref[...]
    y = y_ref[...]
    inv_n = 1.0 / N_HEAD
    mu = _head_sum(y, p, pt) * inv_n
    yc = y - mu
    var = _head_sum(yc * yc, p, pt) * inv_n
    yn = yc * lax.rsqrt(var + GN_EPS) * lxg_ref[...] + lxb_ref[...]
    v = v_ref[...]
    bonus = _head_sum(r_ref[...] * k_ref[...] * rk_ref[...], p, pt) * v
    h = _bdot((yn + bonus) * g_ref[...], wo_ref[...])
    o_ref[...] = _layer_norm(alpha * x_ref[...] + h, lng_ref[...], lnb_ref[...])


def _rwkv_out(y, r, k, v, g, x2d, r_k, lnx_g, lnx_b, w_o, ln_g, ln_b, p, pt, alpha, ts=512):
    t, d = x2d.shape
    row = lambda a: a.reshape(1, d)
    tok = pl.BlockSpec((ts, d), lambda i: (i, 0))
    consts = [row(r_k), row(lnx_g), row(lnx_b), w_o.astype(BF16), row(ln_g), row(ln_b), p, pt]
    return pl.pallas_call(
        functools.partial(_rwkv_out_kernel, alpha=alpha),
        grid=(t // ts,),
        in_specs=[tok] * 6 + [_const_spec(a.shape) for a in consts],
        out_specs=tok,
        out_shape=jax.ShapeDtypeStruct((t, d), F32),
        compiler_params=pltpu.CompilerParams(dimension_semantics=("parallel",),
                                             vmem_limit_bytes=VMEM_LIMIT),
        name="rwkv_out",
    )(y, r, k, v, g, x2d, *consts)


def _ffn_kernel(xp_ref, x_ref, wup_ref, cw_ref, cb_ref, wdn_ref, lng_ref, lnb_ref, o_ref, u_scr,
                *, ts, seq, d_ff, fc, alpha):
    first = (pl.program_id(0) * ts) % seq == 0
    x = x_ref[...]
    xp = jnp.where(first, 0.0, xp_ref[...])
    xe = jnp.concatenate([xp, x], axis=0).astype(BF16)

    def conv_cols(off):
        u_scr[...] = jnp.dot(xe, wup_ref[:, off:off + fc], preferred_element_type=F32)
        cw = cw_ref[:, off:off + fc]
        return (u_scr[SUBLANES - 2:SUBLANES - 2 + ts, :] * cw[0:1]
                + u_scr[SUBLANES - 1:SUBLANES - 1 + ts, :] * cw[1:2]
                + u_scr[SUBLANES:SUBLANES + ts, :] * cw[2:3]
                + cb_ref[:, off:off + fc])

    acc = jnp.zeros(x.shape, F32)
    for c in range(d_ff // fc):
        gate = conv_cols(c * fc)
        val = conv_cols(d_ff + c * fc)
        act = jax.nn.silu(gate) * val
        acc = acc + jnp.dot(act.astype(BF16), wdn_ref[c * fc:(c + 1) * fc, :], preferred_element_type=F32)
    o_ref[...] = _layer_norm(alpha * x + acc, lng_ref[...], lnb_ref[...])


def _ffn(x2d, seq, w_up, conv_w, conv_b, w_down, ln_g, ln_b, alpha, ts=512, fc=1408):
    t, d = x2d.shape
    d_ff = w_down.shape[0]
    tok = pl.BlockSpec((ts, d), lambda i: (i, 0))
    single = pl.Buffered(1)
    wup_spec = pl.BlockSpec((d, 2 * d_ff), lambda i: (0, 0), pipeline_mode=single)
    wdn_spec = pl.BlockSpec((d_ff, d), lambda i: (0, 0), pipeline_mode=single)
    consts = [conv_w, conv_b.reshape(1, -1)]
    rows = [ln_g.reshape(1, d), ln_b.reshape(1, d)]
    return pl.pallas_call(
        functools.partial(_ffn_kernel, ts=ts, seq=seq, d_ff=d_ff, fc=fc, alpha=alpha),
        grid=(t // ts,),
        in_specs=[_prev_rows_spec(ts, d), tok, wup_spec] + [_const_spec(a.shape) for a in consts]
                 + [wdn_spec] + [_const_spec(a.shape) for a in rows],
        out_specs=tok,
        out_shape=jax.ShapeDtypeStruct((t, d), F32),
        scratch_shapes=[pltpu.VMEM((ts + SUBLANES, fc), F32)],
        compiler_params=pltpu.CompilerParams(dimension_semantics=("parallel",),
                                             vmem_limit_bytes=VMEM_LIMIT),
        name="conv_ffn",
    )(x2d, x2d, w_up.astype(BF16), *consts, w_down.astype(BF16), *rows)


def _dsa_proj_kernel(x_ref, wcq_ref, wckv_ref, wki_ref, wwi_ref, qg_ref, kvg_ref, wuq_ref, wukp_ref,
                     wqi_ref, kig_ref, kib_ref, qa_o, qi_o, ki_o, wi_o, ckv_o):
    x = x_ref[...].astype(BF16)
    c_q = jnp.dot(x, wcq_ref[...], preferred_element_type=F32)
    c_kv = jnp.dot(x, wckv_ref[...], preferred_element_type=F32)
    k_idx = jnp.dot(x, wki_ref[...], preferred_element_type=F32)
    w_idx = jnp.dot(x, wwi_ref[...], preferred_element_type=F32)

    def rms(z, g):
        return z * lax.rsqrt(jnp.mean(z * z, axis=-1, keepdims=True) + 1e-6) * g

    c_q = rms(c_q, qg_ref[...]).astype(BF16)
    ckv_o[...] = rms(c_kv, kvg_ref[...]).astype(BF16)
    q = jnp.dot(c_q, wuq_ref[...], preferred_element_type=F32)
    qk_scale = QK_HEAD ** -0.5
    for hp in range(ATT_HEADS // 2):
        qa = _bdot(q[:, hp * LANES:(hp + 1) * LANES], wukp_ref[hp]) * qk_scale
        qa_o[2 * hp] = qa[:, :KV_LORA].astype(BF16)
        qa_o[2 * hp + 1] = qa[:, KV_LORA:].astype(BF16)
    for h in range(IDX_HEADS):
        qi_o[h] = jnp.dot(c_q, wqi_ref[h], preferred_element_type=F32).astype(BF16)
    ki_o[...] = _layer_norm(k_idx, kig_ref[...], kib_ref[...]).astype(BF16)
    wi_o[...] = w_idx * (IDX_HEADS ** -0.5 * IDX_DIM ** -0.5)


def _dsa_proj(x2d, w_in, q_norm_g, kv_norm_g, w_uq, w_uk, w_qidx, kidx_g, kidx_b, ts=512):
    t, d = x2d.shape
    o1, o2, o3 = Q_LORA, Q_LORA + KV_LORA, Q_LORA + KV_LORA + IDX_DIM
    w_in = w_in.astype(BF16)
    z = jnp.zeros((ATT_HEADS // 2, QK_HEAD, KV_LORA), F32)
    wuk_pair = jnp.concatenate([jnp.concatenate([w_uk[0::2], z], axis=2),
                                jnp.concatenate([z, w_uk[1::2]], axis=2)], axis=1).astype(BF16)
    wqi = w_qidx.reshape(Q_LORA, IDX_HEADS, IDX_DIM).transpose(1, 0, 2).astype(BF16)
    consts = [w_in[:, :o1], w_in[:, o1:o2], w_in[:, o2:o3], w_in[:, o3:], q_norm_g.reshape(1, -1),
              kv_norm_g.reshape(1, -1), w_uq.astype(BF16), wuk_pair, wqi, kidx_g.reshape(1, -1),
              kidx_b.reshape(1, -1)]
    out_shape = [jax.ShapeDtypeStruct((ATT_HEADS, t, KV_LORA), BF16),
                 jax.ShapeDtypeStruct((IDX_HEADS, t, IDX_DIM), BF16),
                 jax.ShapeDtypeStruct((t, IDX_DIM), BF16),
                 jax.ShapeDtypeStruct((t, IDX_HEADS), F32),
                 jax.ShapeDtypeStruct((t, KV_LORA), BF16)]
    out_specs = [pl.BlockSpec((ATT_HEADS, ts, KV_LORA), lambda i: (0, i, 0)),
                 pl.BlockSpec((IDX_HEADS, ts, IDX_DIM), lambda i: (0, i, 0)),
                 pl.BlockSpec((ts, IDX_DIM), lambda i: (i, 0)),
                 pl.BlockSpec((ts, IDX_HEADS), lambda i: (i, 0)),
                 pl.BlockSpec((ts, KV_LORA), lambda i: (i, 0))]
    return pl.pallas_call(
        _dsa_proj_kernel,
        grid=(t // ts,),
        in_specs=[pl.BlockSpec((ts, d), lambda i: (i, 0))] + [_const_spec(a.shape) for a in consts],
        out_specs=out_specs,
        out_shape=out_shape,
        compiler_params=pltpu.CompilerParams(dimension_semantics=("parallel",),
                                             vmem_limit_bytes=VMEM_LIMIT),
        name="dsa_proj",
    )(x2d, *consts)


def _bias_tiles_kernel(rb_ref, o_ref, *, qb):
    tl = lax.broadcasted_iota(jnp.int32, (qb, 2 * qb), 0)
    c = lax.broadcasted_iota(jnp.int32, (qb, 2 * qb), 1)
    n = jnp.maximum(qb + tl - c, 0)
    max_exact = REL_BUCKETS // 2
    nf = jnp.maximum(n, 1).astype(F32)
    large = max_exact + (jnp.log(nf / max_exact) / math.log(REL_MAX_DIST / max_exact)
                         * (REL_BUCKETS - max_exact)).astype(jnp.int32)
    large = jnp.minimum(large, REL_BUCKETS - 1)
    bucket = jnp.where(n < max_exact, n, large)
    for h in range(ATT_HEADS):
        acc = jnp.zeros((qb, 2 * qb), F32)
        for bkt in range(REL_BUCKETS):
            acc = jnp.where(bucket == bkt, rb_ref[bkt, h], acc)
        o_ref[h] = acc


def _bias_tiles(rel_bias, qb):
    return pl.pallas_call(
        functools.partial(_bias_tiles_kernel, qb=qb),
        in_specs=[pl.BlockSpec(memory_space=pltpu.SMEM)],
        out_specs=pl.BlockSpec(memory_space=pltpu.VMEM),
        out_shape=jax.ShapeDtypeStruct((ATT_HEADS, qb, 2 * qb), F32),
        name="rel_bias_tiles",
    )(rel_bias)


def _dsa_attn_kernel(far_ref, qi_ref, wi_ref, ki_ref, ckv_ref, qa_ref, near_ref, tri_ref, wuv_ref,
                     wo_ref, x_ref, lng_ref, lnb_ref, o_ref, madd_scr, *, seq, topk, qb, alpha):
    i = pl.program_id(1)
    nblk = seq // qb
    t_pos = i * qb + lax.broadcasted_iota(jnp.int32, (qb, 1), 0)
    s_pos = lax.broadcasted_iota(jnp.int32, (1, seq), 1)
    causal = s_pos <= t_pos

    ki = ki_ref[...]
    wi = wi_ref[...]
    score = jnp.zeros((qb, seq), F32)
    for h in range(IDX_HEADS):
        s = _bdot_nt(qi_ref[h], ki)
        score = score + jnp.maximum(s, 0.0) * wi[:, h:h + 1]

    bits = lax.bitcast_convert_type(score, jnp.int32)
    key = jnp.where(bits < 0, bits ^ jnp.int32(0x7FFFFFFF), bits)
    key = jnp.where(causal, key, jnp.int32(INT_MIN))

    def count_ge(cand):
        return jnp.sum(jnp.where(key >= cand, 1.0, 0.0), axis=1, keepdims=True)

    kf = float(topk)
    tau = jnp.where(count_ge(jnp.zeros((qb, 1), jnp.int32)) >= kf, jnp.int32(0), jnp.int32(INT_MIN))

    def bit_step(bi, tau):
        cand = tau | lax.shift_left(jnp.int32(1), jnp.int32(30) - bi)
        return jnp.where(count_ge(cand) >= kf, cand, tau)

    tau = lax.fori_loop(0, 31, bit_step, tau)

    gt = key > tau
    eqc = jnp.logical_and(key == tau, causal)
    need = kf - jnp.sum(jnp.where(gt, 1.0, 0.0), axis=1, keepdims=True)
    eqf = jnp.where(eqc, 1.0, 0.0)
    tri = tri_ref[...]
    run = jnp.zeros((qb, 1), F32)
    for j in range(nblk):
        sl = slice(j * qb, (j + 1) * qb)
        e = eqf[:, sl]
        rank = jnp.dot(e.astype(BF16), tri, preferred_element_type=F32) + run
        sel = jnp.logical_or(gt[:, sl], jnp.logical_and(eqc[:, sl], rank <= need))
        madd_scr[:, sl] = jnp.where(sel, 0.0, NEG_MASK)
        run = run + jnp.sum(e, axis=1, keepdims=True)

    ckv = ckv_ref[...]

    def head(h, acc):
        near = near_ref[h]
        far = far_ref[h]
        bias = jnp.concatenate(
            [jnp.where(i == j, near[:, qb:], jnp.where(i - 1 == j, near[:, :qb], far)) for j in range(nblk)],
            axis=1)
        l = _bdot_nt(qa_ref[h], ckv) + bias + madd_scr[...]
        m = jnp.max(l, axis=1, keepdims=True)
        p = jnp.exp(l - m)
        den = jnp.sum(p, axis=1, keepdims=True)
        o_lat = jnp.dot(p.astype(BF16), ckv, preferred_element_type=F32) / den
        o_h = _bdot(o_lat, wuv_ref[h])
        return acc + _bdot(o_h, wo_ref[pl.ds(pl.multiple_of(h * V_HEAD, V_HEAD), V_HEAD), :])

    h_out = lax.fori_loop(0, ATT_HEADS, head, jnp.zeros(o_ref.shape, F32))
    o_ref[...] = _layer_norm(alpha * x_ref[...] + h_out, lng_ref[...], lnb_ref[...])


def _dsa_attn(x2d, batch, seq, qa, qi, ki, wi, ckv, near, far, w_uv, w_o, ln_g, ln_b, alpha):
    t, d = x2d.shape
    qb = Q_BLOCK
    nb = seq // qb
    topk = min(TOPK_MAX, seq // 4)
    tri = (jnp.arange(qb)[:, None] <= jnp.arange(qb)[None, :]).astype(BF16)
    tok = lambda b, i: (b * nb + i, 0)
    in_specs = [
        pl.BlockSpec(memory_space=pltpu.SMEM),
        pl.BlockSpec((IDX_HEADS, qb, IDX_DIM), lambda b, i: (0, b * nb + i, 0)),
        pl.BlockSpec((qb, IDX_HEADS), tok),
        pl.BlockSpec((seq, IDX_DIM), lambda b, i: (b, 0)),
        pl.BlockSpec((seq, KV_LORA), lambda b, i: (b, 0)),
        pl.BlockSpec((ATT_HEADS, qb, KV_LORA), lambda b, i: (0, b * nb + i, 0)),
        _const_spec(near.shape),
        _const_spec(tri.shape),
        _const_spec(w_uv.shape),
        _const_spec(w_o.shape),
        pl.BlockSpec((qb, d), tok),
        _const_spec((1, d)),
        _const_spec((1, d)),
    ]
    return pl.pallas_call(
        functools.partial(_dsa_attn_kernel, seq=seq, topk=topk, qb=qb, alpha=alpha),
        grid=(batch, nb),
        in_specs=in_specs,
        out_specs=pl.BlockSpec((qb, d), tok),
        out_shape=jax.ShapeDtypeStruct((t, d), F32),
        scratch_shapes=[pltpu.VMEM((qb, seq), F32)],
        compiler_params=pltpu.CompilerParams(dimension_semantics=("parallel", "parallel"),
                                             vmem_limit_bytes=VMEM_LIMIT),
        name="dsa_attn",
    )(far, qi, wi, ki, ckv, qa, near, tri, w_uv.astype(BF16), w_o.astype(BF16), x2d,
      ln_g.reshape(1, d), ln_b.reshape(1, d))


def kernel(x, ln_g, ln_b, rwkv_mix, rwkv_w_rkv, rwkv_w0, rwkv_w1, rwkv_w2, rwkv_a0, rwkv_a1, rwkv_a2, rwkv_v0, rwkv_v1, rwkv_v2, rwkv_g1, rwkv_g2, rwkv_k_k, rwkv_k_a, rwkv_r_k, rwkv_lnx_g, rwkv_lnx_b, rwkv_w_o, dsa_w_in, dsa_q_norm_g, dsa_kv_norm_g, dsa_w_uq, dsa_w_uk, dsa_w_uv, dsa_w_qidx, dsa_kidx_g, dsa_kidx_b, dsa_w_o, rel_bias, ffn_w_up, ffn_conv_w, ffn_conv_b, ffn_w_down):
    batch, seq, d = x.shape
    depth = ln_g.shape[0]
    heads = d // N_HEAD
    alpha = (2 * depth) ** 0.25
    t = batch * seq
    x2d = x.reshape(t, d)

    p = (jnp.arange(d)[:, None] // N_HEAD == jnp.arange(LANES)[None, :]).astype(F32)
    pt = p.T
    near = _bias_tiles(rel_bias, Q_BLOCK)
    far = rel_bias[REL_BUCKETS - 1]

    def to_scan(z):
        return z.reshape(batch, seq, heads, N_HEAD).transpose(1, 3, 0, 2).reshape(seq, N_HEAD, batch * heads)

    def from_scan(z):
        return z.reshape(seq, N_HEAD, batch, heads).transpose(2, 0, 3, 1).reshape(t, d)

    v_first = None
    for i in range(depth):
        j = i // 2
        if i % 2 == 0:
            vres = None if j == 0 else (v_first, rwkv_v0[j - 1], rwkv_v1[j - 1], rwkv_v2[j - 1])
            r, w, k, v, a_s, b_s, g = _rwkv_proj(
                x2d, seq, rwkv_mix[j], rwkv_w_rkv[j], rwkv_w0[j], rwkv_w1[j], rwkv_w2[j], rwkv_a0[j],
                rwkv_a1[j], rwkv_a2[j], rwkv_g1[j], rwkv_g2[j], rwkv_k_k[j], rwkv_k_a[j], p, pt, vres)
            if j == 0:
                v_first = v
            y = _wkv(to_scan(r), to_scan(w), to_scan(k), to_scan(v), to_scan(a_s), to_scan(b_s))
            x2d = _rwkv_out(from_scan(y), r, k, v, g, x2d, rwkv_r_k[j], rwkv_lnx_g[j], rwkv_lnx_b[j],
                            rwkv_w_o[j], ln_g[i, 0], ln_b[i, 0], p, pt, alpha)
        else:
            qa, qi, ki, wi, ckv = _dsa_proj(x2d, dsa_w_in[j], dsa_q_norm_g[j], dsa_kv_norm_g[j],
                                            dsa_w_uq[j], dsa_w_uk[j], dsa_w_qidx[j], dsa_kidx_g[j],
                                            dsa_kidx_b[j])
            x2d = _dsa_attn(x2d, batch, seq, qa, qi, ki, wi, ckv, near, far, dsa_w_uv[j], dsa_w_o[j],
                            ln_g[i, 0], ln_b[i, 0], alpha)
        x2d = _ffn(x2d, seq, ffn_w_up[i], ffn_conv_w[i], ffn_conv_b[i], ffn_w_down[i],
                   ln_g[i, 1], ln_b[i, 1], alpha)
    return x2d.reshape(batch, seq, d)
```

```python
import functools
import math

import jax
import jax.numpy as jnp
from jax import lax
from jax.experimental import pallas as pl
from jax.experimental.pallas import tpu as pltpu

F32 = jnp.float32
BF16 = jnp.bfloat16

N_HEAD = 64
ATT_HEADS = 16
QK_HEAD = 64
V_HEAD = 64
Q_LORA = 256
KV_LORA = 128
IDX_HEADS = 8
IDX_DIM = 64
TOPK_MAX = 256
DSA_ROWS = 256
REL_BUCKETS = 32
REL_MAX_DIST = 128
CONV_W = 3
GN_EPS = 64e-5
LN_EPS = 1e-5

LANES = 128
SUBLANES = 8
VMEM_LIMIT = 56 * 1024 * 1024

NEG_MASK = -1e30
INT_MIN = -2 ** 31


def _bdot(a, b):
    return jnp.dot(a.astype(BF16), b.astype(BF16), preferred_element_type=F32)


def _bdot_nt(a, b):
    return lax.dot_general(a.astype(BF16), b.astype(BF16), (((1,), (1,)), ((), ())),
                           preferred_element_type=F32)


def _split_dot(a, b):
    hi = a.astype(BF16)
    lo = (a - hi.astype(F32)).astype(BF16)
    return (jnp.dot(hi, b, preferred_element_type=F32) + jnp.dot(lo, b, preferred_element_type=F32))


def _layer_norm(x, g, b):
    mu = jnp.mean(x, axis=-1, keepdims=True)
    xc = x - mu
    var = jnp.mean(xc * xc, axis=-1, keepdims=True)
    return xc * lax.rsqrt(var + LN_EPS) * g + b


def _head_sum(x, p, pt):
    return _split_dot(_split_dot(x, p), pt)


def _const_spec(shape):
    nd = len(shape)
    return pl.BlockSpec(shape, lambda *_: (0,) * nd)


def _prev_rows_spec(ts, d):
    return pl.BlockSpec((SUBLANES, d), lambda i: (jnp.maximum(i * (ts // SUBLANES) - 1, 0), 0))


def _rwkv_proj_kernel(*refs, ts, seq, has_vres):
    (xp_ref, x_ref, mix_ref, wr_ref, wk_ref, wv_ref, w0_ref, w1_ref, w2_ref, a0_ref, a1_ref,
     a2_ref, g1_ref, g2_ref, kk_ref, ka_ref, p_ref, pt_ref) = refs[:18]
    if has_vres:
        vf_ref, v0_ref, v1_ref, v2_ref = refs[18:22]
        outs = refs[22:]
    else:
        outs = refs[18:]
    r_o, w_o, k_o, v_o, a_o, b_o, g_o = outs

    first = (pl.program_id(0) * ts) % seq == 0
    x = x_ref[...]
    p_last = jnp.where(first, 0.0, xp_ref[SUBLANES - 1:SUBLANES, :])
    rows = lax.broadcasted_iota(jnp.int32, (ts, 1), 0)
    x_shift = jnp.where(rows == 0, p_last, pltpu.roll(x, 1, 0))
    xx = x_shift - x
    mix = mix_ref[...]

    def xm(j):
        return x + xx * mix[j:j + 1]

    r = _bdot(xm(0), wr_ref[...])
    k = _bdot(xm(1), wk_ref[...])
    xv = xm(2)
    v = _bdot(xv, wv_ref[...])
    w_log = -jax.nn.softplus(-(w0_ref[...] + _bdot(jnp.tanh(_bdot(xm(3), w1_ref[...])), w2_ref[...]))) - 0.5
    decay = jnp.exp(-jnp.exp(w_log))
    if has_vres:
        v = v + (vf_ref[...] - v) * jax.nn.sigmoid(v0_ref[...] + _bdot(_bdot(xv, v1_ref[...]), v2_ref[...]))
    a = jax.nn.sigmoid(a0_ref[...] + _bdot(_bdot(xm(4), a1_ref[...]), a2_ref[...]))
    g = _bdot(jax.nn.sigmoid(_bdot(xm(5), g1_ref[...])), g2_ref[...])
    kk = k * kk_ref[...]
    ss = _head_sum(kk * kk, p_ref[...], pt_ref[...])
    kk = kk / jnp.maximum(jnp.sqrt(ss), 1e-12)
    k = k * (1.0 + (a - 1.0) * ka_ref[...])
    r_o[...] = r
    w_o[...] = decay
    k_o[...] = k
    v_o[...] = v
    a_o[...] = -kk
    b_o[...] = kk * a
    g_o[...] = g


def _rwkv_proj(x2d, seq, mix, w_rkv, w0, w1, w2, a0, a1, a2, g1, g2, k_k, k_a, p, pt, vres, ts=256):
    t, d = x2d.shape
    row = lambda v: v.reshape(1, d)
    tok = pl.BlockSpec((ts, d), lambda i: (i, 0))
    ins = [x2d, x2d, mix, w_rkv[0].astype(BF16), w_rkv[1].astype(BF16), w_rkv[2].astype(BF16),
           row(w0), w1.astype(BF16), w2.astype(BF16), row(a0), a1.astype(BF16), a2.astype(BF16),
           g1.astype(BF16), g2.astype(BF16), row(k_k), row(k_a), p, pt]
    specs = [_prev_rows_spec(ts, d), tok] + [_const_spec(a.shape) for a in ins[2:]]
    if vres is not None:
        v_first, v0, v1, v2 = vres
        extra = [v_first, row(v0), v1.astype(BF16), v2.astype(BF16)]
        ins += extra
        specs += [tok] + [_const_spec(a.shape) for a in extra[1:]]
    out_shape = [jax.ShapeDtypeStruct((t, d), F32)] * 7
    return pl.pallas_call(
        functools.partial(_rwkv_proj_kernel, ts=ts, seq=seq, has_vres=vres is not None),
        grid=(t // ts,),
        in_specs=specs,
        out_specs=[tok] * 7,
        out_shape=out_shape,
        compiler_params=pltpu.CompilerParams(dimension_semantics=("parallel",),
                                             vmem_limit_bytes=VMEM_LIMIT),
        name="rwkv_proj",
    )(*ins)


def _wkv_kernel(r_ref, w_ref, k_ref, v_ref, a_ref, b_ref, y_ref, s_scr, *, tc, n):
    @pl.when(pl.program_id(1) == 0)
    def _():
        s_scr[...] = jnp.zeros_like(s_scr)

    def step(t, carry):
        a = a_ref[t]
        w = w_ref[t]
        b = b_ref[t]
        k = k_ref[t]
        r = r_ref[t]

        def vloop(vi, c):
            s = s_scr[vi]
            sa = jnp.sum(s * a, axis=0, keepdims=True)
            vv = v_ref[t, pl.ds(vi, 1), :]
            s_new = s * w + sa * b + vv * k
            s_scr[vi] = s_new
            y_ref[t, pl.ds(vi, 1), :] = jnp.sum(s_new * r, axis=0, keepdims=True)
            return c

        return lax.fori_loop(0, n, vloop, carry, unroll=4)

    lax.fori_loop(0, tc, step, 0)


def _wkv(r, w, k, v, a, b, tc=16):
    s, n, pairs = r.shape
    spec = pl.BlockSpec((tc, n, LANES), lambda p, c: (c, 0, p))
    return pl.pallas_call(
        functools.partial(_wkv_kernel, tc=tc, n=n),
        grid=(pairs // LANES, s // tc),
        in_specs=[spec] * 6,
        out_specs=spec,
        out_shape=jax.ShapeDtypeStruct((s, n, pairs), F32),
        scratch_shapes=[pltpu.VMEM((n, n, LANES), F32)],
        compiler_params=pltpu.CompilerParams(dimension_semantics=("parallel", "arbitrary"),
                                             vmem_limit_bytes=VMEM_LIMIT),
        name="wkv7_scan",
    )(r, w, k, v, a, b)


def _rwkv_out_kernel(y_ref, r_ref, k_ref, v_ref, g_ref, x_ref, rk_ref, lxg_ref, lxb_ref, wo_ref,
                     lng_ref, lnb_ref, p_ref, pt_ref, o_ref, *, alpha):
    p = p_ref[...]
    pt = pt_ref[...]
    y = y_ref[...]
    inv_n = 1.0 / N_HEAD
    mu = _head_sum(y, p, pt) * inv_n
    yc = y - mu
    var = _head_sum(yc * yc, p, pt) * inv_n
    yn = yc * lax.rsqrt(var + GN_EPS) * lxg_ref[...] + lxb_ref[...]
    v = v_ref[...]
    bonus = _head_sum(r_ref[...] * k_ref[...] * rk_ref[...], p, pt) * v
    h = _bdot((yn + bonus) * g_ref[...], wo_ref[...])
    o_ref[...] = _layer_norm(alpha * x_ref[...] + h, lng_ref[...], lnb_ref[...])


def _rwkv_out(y, r, k, v, g, x2d, r_k, lnx_g, lnx_b, w_o, ln_g, ln_b, p, pt, alpha, ts=512):
    t, d = x2d.shape
    row = lambda a: a.reshape(1, d)
    tok = pl.BlockSpec((ts, d), lambda i: (i, 0))
    consts = [row(r_k), row(lnx_g), row(lnx_b), w_o.astype(BF16), row(ln_g), row(ln_b), p, pt]
    return pl.pallas_call(
        functools.partial(_rwkv_out_kernel, alpha=alpha),
        grid=(t // ts,),
        in_specs=[tok] * 6 + [_const_spec(a.shape) for a in consts],
        out_specs=tok,
        out_shape=jax.ShapeDtypeStruct((t, d), F32),
        compiler_params=pltpu.CompilerParams(dimension_semantics=("parallel",),
                                             vmem_limit_bytes=VMEM_LIMIT),
        name="rwkv_out",
    )(y, r, k, v, g, x2d, *consts)


def _ffn_kernel(xp_ref, x_ref, wup_ref, cw_ref, cb_ref, wdn_ref, lng_ref, lnb_ref, o_ref, u_scr,
                *, ts, seq, d_ff, fc, alpha):
    first = (pl.program_id(0) * ts) % seq == 0
    x = x_ref[...]
    xp = jnp.where(first, 0.0, xp_ref[...])
    xe = jnp.concatenate([xp, x], axis=0).astype(BF16)

    def conv_cols(off):
        u_scr[...] = jnp.dot(xe, wup_ref[:, off:off + fc], preferred_element_type=F32)
        cw = cw_ref[:, off:off + fc]
        return (u_scr[SUBLANES - 2:SUBLANES - 2 + ts, :] * cw[0:1]
                + u_scr[SUBLANES - 1:SUBLANES - 1 + ts, :] * cw[1:2]
                + u_scr[SUBLANES:SUBLANES + ts, :] * cw[2:3]
                + cb_ref[:, off:off + fc])

    acc = jnp.zeros(x.shape, F32)
    for c in range(d_ff // fc):
        gate = conv_cols(c * fc)
        val = conv_cols(d_ff + c * fc)
        act = jax.nn.silu(gate) * val
        acc = acc + jnp.dot(act.astype(BF16), wdn_ref[c * fc:(c + 1) * fc, :], preferred_element_type=F32)
    o_ref[...] = _layer_norm(alpha * x + acc, lng_ref[...], lnb_ref[...])


def _ffn(x2d, seq, w_up, conv_w, conv_b, w_down, ln_g, ln_b, alpha, ts=512, fc=1408):
    t, d = x2d.shape
    d_ff = w_down.shape[0]
    tok = pl.BlockSpec((ts, d), lambda i: (i, 0))
    single = pl.Buffered(1)
    wup_spec = pl.BlockSpec((d, 2 * d_ff), lambda i: (0, 0), pipeline_mode=single)
    wdn_spec = pl.BlockSpec((d_ff, d), lambda i: (0, 0), pipeline_mode=single)
    consts = [conv_w, conv_b.reshape(1, -1)]
    rows = [ln_g.reshape(1, d), ln_b.reshape(1, d)]
    return pl.pallas_call(
        functools.partial(_ffn_kernel, ts=ts, seq=seq, d_ff=d_ff, fc=fc, alpha=alpha),
        grid=(t // ts,),
        in_specs=[_prev_rows_spec(ts, d), tok, wup_spec] + [_const_spec(a.shape) for a in consts]
                 + [wdn_spec] + [_const_spec(a.shape) for a in rows],
        out_specs=tok,
        out_shape=jax.ShapeDtypeStruct((t, d), F32),
        scratch_shapes=[pltpu.VMEM((ts + SUBLANES, fc), F32)],
        compiler_params=pltpu.CompilerParams(dimension_semantics=("parallel",),
                                             vmem_limit_bytes=VMEM_LIMIT),
        name="conv_ffn",
    )(x2d, x2d, w_up.astype(BF16), *consts, w_down.astype(BF16), *rows)


def _dsa_proj_kernel(x_ref, wcq_ref, wckv_ref, wki_ref, wwit_ref, qg_ref, kvg_ref, wuq_ref, wukp_ref,
                     wqi_ref, kig_ref, kib_ref, qa_o, qi_o, ki_o, wit_o, ckv_o, ckvt_o, *, cw):
    x = x_ref[...].astype(BF16)
    c_q = jnp.dot(x, wcq_ref[...], preferred_element_type=F32)
    c_kv = jnp.dot(x, wckv_ref[...], preferred_element_type=F32)
    k_idx = jnp.dot(x, wki_ref[...], preferred_element_type=F32)
    w_idx_t = _bdot_nt(wwit_ref[...], x)

    def rms(z, g):
        return z * lax.rsqrt(jnp.mean(z * z, axis=-1, keepdims=True) + 1e-6) * g

    c_q = rms(c_q, qg_ref[...]).astype(BF16)
    c_kv = rms(c_kv, kvg_ref[...])
    ckv_o[...] = c_kv.astype(BF16)
    for jc in range(c_kv.shape[0] // cw):
        ckvt_o[jc] = c_kv[jc * cw:(jc + 1) * cw].T.astype(BF16)
    q = jnp.dot(c_q, wuq_ref[...], preferred_element_type=F32)
    qk_scale = QK_HEAD ** -0.5
    for hp in range(ATT_HEADS // 2):
        qa = _bdot(q[:, hp * LANES:(hp + 1) * LANES], wukp_ref[hp]) * qk_scale
        qa_o[2 * hp] = qa[:, :KV_LORA].astype(BF16)
        qa_o[2 * hp + 1] = qa[:, KV_LORA:].astype(BF16)
    for h in range(IDX_HEADS):
        qi_o[h] = jnp.dot(c_q, wqi_ref[h], preferred_element_type=F32).astype(BF16)
    ki_o[...] = _layer_norm(k_idx, kig_ref[...], kib_ref[...]).astype(BF16)
    wit_o[...] = w_idx_t * (IDX_HEADS ** -0.5 * IDX_DIM ** -0.5)


def _dsa_proj(x2d, w_in, q_norm_g, kv_norm_g, w_uq, w_uk, w_qidx, kidx_g, kidx_b, ts=512):
    t, d = x2d.shape
    cw = DSA_ROWS
    o1, o2, o3 = Q_LORA, Q_LORA + KV_LORA, Q_LORA + KV_LORA + IDX_DIM
    w_in = w_in.astype(BF16)
    z = jnp.zeros((ATT_HEADS // 2, QK_HEAD, KV_LORA), F32)
    wuk_pair = jnp.concatenate([jnp.concatenate([w_uk[0::2], z], axis=2),
                                jnp.concatenate([z, w_uk[1::2]], axis=2)], axis=1).astype(BF16)
    wqi = w_qidx.reshape(Q_LORA, IDX_HEADS, IDX_DIM).transpose(1, 0, 2).astype(BF16)
    consts = [w_in[:, :o1], w_in[:, o1:o2], w_in[:, o2:o3], w_in[:, o3:].T, q_norm_g.reshape(1, -1),
              kv_norm_g.reshape(1, -1), w_uq.astype(BF16), wuk_pair, wqi, kidx_g.reshape(1, -1),
              kidx_b.reshape(1, -1)]
    out_shape = [jax.ShapeDtypeStruct((ATT_HEADS, t, KV_LORA), BF16),
                 jax.ShapeDtypeStruct((IDX_HEADS, t, IDX_DIM), BF16),
                 jax.ShapeDtypeStruct((t, IDX_DIM), BF16),
                 jax.ShapeDtypeStruct((IDX_HEADS, t), F32),
                 jax.ShapeDtypeStruct((t, KV_LORA), BF16),
                 jax.ShapeDtypeStruct((t // cw, KV_LORA, cw), BF16)]
    out_specs = [pl.BlockSpec((ATT_HEADS, ts, KV_LORA), lambda i: (0, i, 0)),
                 pl.BlockSpec((IDX_HEADS, ts, IDX_DIM), lambda i: (0, i, 0)),
                 pl.BlockSpec((ts, IDX_DIM), lambda i: (i, 0)),
                 pl.BlockSpec((IDX_HEADS, ts), lambda i: (0, i)),
                 pl.BlockSpec((ts, KV_LORA), lambda i: (i, 0)),
                 pl.BlockSpec((ts // cw, KV_LORA, cw), lambda i: (i, 0, 0))]
    return pl.pallas_call(
        functools.partial(_dsa_proj_kernel, cw=cw),
        grid=(t // ts,),
        in_specs=[pl.BlockSpec((ts, d), lambda i: (i, 0))] + [_const_spec(a.shape) for a in consts],
        out_specs=out_specs,
        out_shape=out_shape,
        compiler_params=pltpu.CompilerParams(dimension_semantics=("parallel",),
                                             vmem_limit_bytes=VMEM_LIMIT),
        name="dsa_proj",
    )(x2d, *consts)


def _bias_tiles_kernel(rb_ref, o_ref, *, qb):
    c = lax.broadcasted_iota(jnp.int32, (2 * qb, qb), 0)
    tl = lax.broadcasted_iota(jnp.int32, (2 * qb, qb), 1)
    n = jnp.maximum(qb + tl - c, 0)
    max_exact = REL_BUCKETS // 2
    nf = jnp.maximum(n, 1).astype(F32)
    large = max_exact + (jnp.log(nf / max_exact) / math.log(REL_MAX_DIST / max_exact)
                         * (REL_BUCKETS - max_exact)).astype(jnp.int32)
    large = jnp.minimum(large, REL_BUCKETS - 1)
    bucket = jnp.where(n < max_exact, n, large)
    for h in range(ATT_HEADS):
        acc = jnp.zeros((2 * qb, qb), F32)
        for bkt in range(REL_BUCKETS):
            acc = jnp.where(bucket == bkt, rb_ref[bkt, h], acc)
        o_ref[h] = acc - rb_ref[REL_BUCKETS - 1, h]


def _bias_tiles(rel_bias, qb):
    return pl.pallas_call(
        functools.partial(_bias_tiles_kernel, qb=qb),
        in_specs=[pl.BlockSpec(memory_space=pltpu.SMEM)],
        out_specs=pl.BlockSpec(memory_space=pltpu.VMEM),
        out_shape=jax.ShapeDtypeStruct((ATT_HEADS, 2 * qb, qb), F32),
        name="rel_bias_tiles",
    )(rel_bias)


def _dsa_attn_kernel(qi_ref, wit_ref, ki_ref, ckv_ref, ckvt_ref, qa_ref, near_ref, tril_ref, wuvt_ref,
                     wo_ref, x_ref, lng_ref, lnb_ref, o_ref, key_scr, madd_scr, m_scr, den_scr, acc_scr,
                     *, topk, qb, alpha):
    cw = qb
    i = pl.program_id(1)
    nch = i + 1
    t_pos = i * qb + lax.broadcasted_iota(jnp.int32, (1, qb), 1)
    s_loc = lax.broadcasted_iota(jnp.int32, (cw, 1), 0)
    int_min = jnp.int32(INT_MIN)

    qi_all = qi_ref[...].reshape(IDX_HEADS * qb, IDX_DIM)
    wit = wit_ref[...]

    def score_chunk(c, carry):
        off = pl.multiple_of(c * cw, cw)
        s_all = _bdot_nt(ki_ref[pl.ds(off, cw), :], qi_all)
        score = jnp.zeros((cw, qb), F32)
        for h in range(IDX_HEADS):
            score = score + jnp.maximum(s_all[:, h * qb:(h + 1) * qb], 0.0) * wit[h:h + 1]
        bits = lax.bitcast_convert_type(score, jnp.int32)
        key = jnp.where(bits < 0, bits ^ jnp.int32(0x7FFFFFFF), bits)
        key_scr[c] = jnp.where(off + s_loc <= t_pos, key, int_min)
        return carry

    lax.fori_loop(0, nch, score_chunk, 0)

    def count(pred):
        def body(c, acc):
            kc = key_scr[c]
            for r in range(cw // SUBLANES):
                acc = acc + jnp.where(pred(kc[r * SUBLANES:(r + 1) * SUBLANES]), 1.0, 0.0)
            return acc

        acc = lax.fori_loop(0, nch, body, jnp.zeros((SUBLANES, qb), F32))
        return jnp.sum(acc, axis=0, keepdims=True)

    kf = float(topk)
    tau = jnp.where(count(lambda kc: kc >= 0) >= kf, jnp.int32(0), int_min)

    def bit_step(bi, tau):
        cand = tau | lax.shift_left(jnp.int32(1), jnp.int32(30) - bi)
        return jnp.where(count(lambda kc: kc >= cand) >= kf, cand, tau)

    tau = lax.fori_loop(0, 31, bit_step, tau)

    need = kf - count(lambda kc: kc > tau)
    has_kth = tau > int_min
    tril = tril_ref[...]

    def select_chunk(c, run):
        kc = key_scr[c]
        eq = jnp.logical_and(kc == tau, has_kth)
        e = jnp.where(eq, 1.0, 0.0)
        rank = jnp.dot(tril, e.astype(BF16), preferred_element_type=F32) + run
        sel = jnp.logical_or(kc > tau, jnp.logical_and(eq, rank <= need))
        madd_scr[c] = jnp.where(sel, 0.0, NEG_MASK)
        return run + jnp.sum(e, axis=0, keepdims=True)

    lax.fori_loop(0, nch, select_chunk, jnp.zeros((1, qb), F32))

    m_scr[...] = jnp.full(m_scr.shape, -jnp.inf, F32)
    den_scr[...] = jnp.zeros(den_scr.shape, F32)
    acc_scr[...] = jnp.zeros(acc_scr.shape, F32)

    def att_chunk(c, with_bias):
        off = pl.multiple_of(c * cw, cw)
        ckv_c = ckv_ref[pl.ds(off, cw), :]
        ckvt_c = ckvt_ref[c]
        boff = pl.multiple_of(jnp.where(c == i, qb, 0), qb)

        def head(h, carry):
            l = _bdot_nt(ckv_c, qa_ref[h]) + madd_scr[c]
            if with_bias:
                l = l + near_ref[h, pl.ds(boff, qb), :]
            m_old = m_scr[h]
            m_new = jnp.maximum(m_old, jnp.max(l, axis=0, keepdims=True))
            p = jnp.exp(l - m_new)
            scale = jnp.exp(m_old - m_new)
            den_scr[h] = den_scr[h] * scale + jnp.sum(p, axis=0, keepdims=True)
            m_scr[h] = m_new
            acc_scr[h] = acc_scr[h] * scale + jnp.dot(ckvt_c, p.astype(BF16), preferred_element_type=F32)
            return carry

        for h in range(ATT_HEADS):
            head(h, 0)

    def far_body(c, carry):
        att_chunk(c, False)
        return carry

    def near_body(c, carry):
        att_chunk(c, True)
        return carry

    c_near = jnp.maximum(i - 1, 0)
    lax.fori_loop(0, c_near, far_body, 0)
    lax.fori_loop(c_near, nch, near_body, 0)

    o_rows = []
    for h in range(ATT_HEADS):
        o_lat_t = acc_scr[h] / den_scr[h]
        o_rows.append(_bdot(wuvt_ref[h], o_lat_t))
    h_out = _bdot(jnp.concatenate(o_rows, axis=0).T, wo_ref[...])
    o_ref[...] = _layer_norm(alpha * x_ref[...] + h_out, lng_ref[...], lnb_ref[...])


def _dsa_attn(x2d, batch, seq, qa, qi, ki, wit, ckv, ckvt, near, w_uv, w_o, ln_g, ln_b, alpha):
    t, d = x2d.shape
    qb = DSA_ROWS
    assert KV_LORA == LANES
    nb = seq // qb
    topk = min(TOPK_MAX, seq // 4)
    tril = (jnp.arange(qb)[:, None] >= jnp.arange(qb)[None, :]).astype(BF16)
    wuv_t = w_uv.transpose(0, 2, 1).astype(BF16)
    wo = w_o.astype(BF16)
    tok = lambda b, i: (b * nb + i, 0)
    in_specs = [
        pl.BlockSpec((IDX_HEADS, qb, IDX_DIM), lambda b, i: (0, b * nb + i, 0)),
        pl.BlockSpec((IDX_HEADS, qb), lambda b, i: (0, b * nb + i)),
        pl.BlockSpec((seq, IDX_DIM), lambda b, i: (b, 0)),
        pl.BlockSpec((seq, KV_LORA), lambda b, i: (b, 0)),
        pl.BlockSpec((nb, KV_LORA, qb), lambda b, i: (b, 0, 0)),
        pl.BlockSpec((ATT_HEADS, qb, KV_LORA), lambda b, i: (0, b * nb + i, 0)),
        pl.BlockSpec(near.shape, lambda b, i: (0, 0, 0), pipeline_mode=pl.Buffered(1)),
        _const_spec(tril.shape),
        _const_spec(wuv_t.shape),
        _const_spec(wo.shape),
        pl.BlockSpec((qb, d), tok),
        _const_spec((1, d)),
        _const_spec((1, d)),
    ]
    scratch = [pltpu.VMEM((nb, qb, qb), jnp.int32),
               pltpu.VMEM((nb, qb, qb), F32),
               pltpu.VMEM((ATT_HEADS, 1, qb), F32),
               pltpu.VMEM((ATT_HEADS, 1, qb), F32),
               pltpu.VMEM((ATT_HEADS, KV_LORA, qb), F32)]
    return pl.pallas_call(
        functools.partial(_dsa_attn_kernel, topk=topk, qb=qb, alpha=alpha),
        grid=(batch, nb),
        in_specs=in_specs,
        out_specs=pl.BlockSpec((qb, d), tok),
        out_shape=jax.ShapeDtypeStruct((t, d), F32),
        scratch_shapes=scratch,
        compiler_params=pltpu.CompilerParams(dimension_semantics=("parallel", "parallel"),
                                             vmem_limit_bytes=VMEM_LIMIT),
        name="dsa_attn",
    )(qi, wit, ki, ckv, ckvt, qa, near, tril, wuv_t, wo, x2d, ln_g.reshape(1, d), ln_b.reshape(1, d))


def kernel(x, ln_g, ln_b, rwkv_mix, rwkv_w_rkv, rwkv_w0, rwkv_w1, rwkv_w2, rwkv_a0, rwkv_a1, rwkv_a2, rwkv_v0, rwkv_v1, rwkv_v2, rwkv_g1, rwkv_g2, rwkv_k_k, rwkv_k_a, rwkv_r_k, rwkv_lnx_g, rwkv_lnx_b, rwkv_w_o, dsa_w_in, dsa_q_norm_g, dsa_kv_norm_g, dsa_w_uq, dsa_w_uk, dsa_w_uv, dsa_w_qidx, dsa_kidx_g, dsa_kidx_b, dsa_w_o, rel_bias, ffn_w_up, ffn_conv_w, ffn_conv_b, ffn_w_down):
    batch, seq, d = x.shape
    depth = ln_g.shape[0]
    heads = d // N_HEAD
    alpha = (2 * depth) ** 0.25
    gb = LANES // heads
    groups = batch // gb
    t = gb * seq
    xs = [x[g * gb:(g + 1) * gb].reshape(t, d) for g in range(groups)]

    p = (jnp.arange(d)[:, None] // N_HEAD == jnp.arange(LANES)[None, :]).astype(BF16)
    pt = p.T
    near = _bias_tiles(rel_bias, DSA_ROWS)

    def to_scan(z):
        return z.reshape(gb, seq, heads, N_HEAD).transpose(1, 3, 0, 2).reshape(seq, N_HEAD, gb * heads)

    def from_scan(z):
        return z.reshape(seq, N_HEAD, gb, heads).transpose(2, 0, 3, 1).reshape(t, d)

    v_first = [None] * groups
    for i in range(depth):
        j = i // 2
        for g in range(groups):
            x2d = xs[g]
            if i % 2 == 0:
                vres = None if j == 0 else (v_first[g], rwkv_v0[j - 1], rwkv_v1[j - 1], rwkv_v2[j - 1])
                r, w, k, v, a_s, b_s, gate = _rwkv_proj(
                    x2d, seq, rwkv_mix[j], rwkv_w_rkv[j], rwkv_w0[j], rwkv_w1[j], rwkv_w2[j], rwkv_a0[j],
                    rwkv_a1[j], rwkv_a2[j], rwkv_g1[j], rwkv_g2[j], rwkv_k_k[j], rwkv_k_a[j], p, pt, vres)
                if j == 0:
                    v_first[g] = v
                y = _wkv(to_scan(r), to_scan(w), to_scan(k), to_scan(v), to_scan(a_s), to_scan(b_s))
                x2d = _rwkv_out(from_scan(y), r, k, v, gate, x2d, rwkv_r_k[j], rwkv_lnx_g[j], rwkv_lnx_b[j],
                                rwkv_w_o[j], ln_g[i, 0], ln_b[i, 0], p, pt, alpha)
            else:
                qa, qi, ki, wit, ckv, ckvt = _dsa_proj(x2d, dsa_w_in[j], dsa_q_norm_g[j], dsa_kv_norm_g[j],
                                                       dsa_w_uq[j], dsa_w_uk[j], dsa_w_qidx[j], dsa_kidx_g[j],
                                                       dsa_kidx_b[j])
                x2d = _dsa_attn(x2d, gb, seq, qa, qi, ki, wit, ckv, ckvt, near, dsa_w_uv[j], dsa_w_o[j],
                                ln_g[i, 0], ln_b[i, 0], alpha)
            xs[g] = _ffn(x2d, seq, ffn_w_up[i], ffn_conv_w[i], ffn_conv_b[i], ffn_w_down[i],
                         ln_g[i, 1], ln_b[i, 1], alpha)
    return jnp.concatenate([z.reshape(gb, seq, d) for z in xs], axis=0)
```

```python
import functools
import math

import jax
import jax.numpy as jnp
from jax import lax
from jax.experimental import pallas as pl
from jax.experimental.pallas import tpu as pltpu

F32 = jnp.float32
BF16 = jnp.bfloat16

N_HEAD = 64
ATT_HEADS = 16
QK_HEAD = 64
V_HEAD = 64
Q_LORA = 256
KV_LORA = 128
IDX_HEADS = 8
IDX_DIM = 64
TOPK_MAX = 256
DSA_ROWS = 256
REL_BUCKETS = 32
REL_MAX_DIST = 128
CONV_W = 3
GN_EPS = 64e-5
LN_EPS = 1e-5

LANES = 128
SUBLANES = 8
VMEM_LIMIT = 56 * 1024 * 1024
WKV_UNROLL = 16

NEG_MASK = -1e30
INT_MIN = -2 ** 31


def _bdot(a, b):
    return jnp.dot(a.astype(BF16), b.astype(BF16), preferred_element_type=F32)


def _bdot_nt(a, b):
    return lax.dot_general(a.astype(BF16), b.astype(BF16), (((1,), (1,)), ((), ())),
                           preferred_element_type=F32)


def _split_dot(a, b):
    hi = a.astype(BF16)
    lo = (a - hi.astype(F32)).astype(BF16)
    return (jnp.dot(hi, b, preferred_element_type=F32) + jnp.dot(lo, b, preferred_element_type=F32))


def _layer_norm(x, g, b):
    mu = jnp.mean(x, axis=-1, keepdims=True)
    xc = x - mu
    var = jnp.mean(xc * xc, axis=-1, keepdims=True)
    return xc * lax.rsqrt(var + LN_EPS) * g + b


def _head_sum(x, p, pt):
    return _split_dot(_split_dot(x, p), pt)


def _const_spec(shape):
    nd = len(shape)
    return pl.BlockSpec(shape, lambda *_: (0,) * nd)


def _prev_rows_spec(ts, d):
    return pl.BlockSpec((SUBLANES, d), lambda i: (jnp.maximum(i * (ts // SUBLANES) - 1, 0), 0))


def _rwkv_proj_kernel(*refs, ts, seq, has_vres):
    (xp_ref, x_ref, mix_ref, wr_ref, wk_ref, wv_ref, w0_ref, w1_ref, w2_ref, a0_ref, a1_ref,
     a2_ref, g1_ref, g2_ref, kk_ref, ka_ref, p_ref, pt_ref) = refs[:18]
    if has_vres:
        vf_ref, v0_ref, v1_ref, v2_ref = refs[18:22]
        outs = refs[22:]
    else:
        outs = refs[18:]
    r_o, w_o, k_o, v_o, a_o, b_o, g_o = outs

    first = (pl.program_id(0) * ts) % seq == 0
    x = x_ref[...]
    p_last = jnp.where(first, 0.0, xp_ref[SUBLANES - 1:SUBLANES, :])
    rows = lax.broadcasted_iota(jnp.int32, (ts, 1), 0)
    x_shift = jnp.where(rows == 0, p_last, pltpu.roll(x, 1, 0))
    xx = x_shift - x
    mix = mix_ref[...]

    def xm(j):
        return x + xx * mix[j:j + 1]

    r = _bdot(xm(0), wr_ref[...])
    k = _bdot(xm(1), wk_ref[...])
    xv = xm(2)
    v = _bdot(xv, wv_ref[...])
    w_log = -jax.nn.softplus(-(w0_ref[...] + _bdot(jnp.tanh(_bdot(xm(3), w1_ref[...])), w2_ref[...]))) - 0.5
    decay = jnp.exp(-jnp.exp(w_log))
    if has_vres:
        v = v + (vf_ref[...] - v) * jax.nn.sigmoid(v0_ref[...] + _bdot(_bdot(xv, v1_ref[...]), v2_ref[...]))
    a = jax.nn.sigmoid(a0_ref[...] + _bdot(_bdot(xm(4), a1_ref[...]), a2_ref[...]))
    g = _bdot(jax.nn.sigmoid(_bdot(xm(5), g1_ref[...])), g2_ref[...])
    kk = k * kk_ref[...]
    ss = _head_sum(kk * kk, p_ref[...], pt_ref[...])
    kk = kk / jnp.maximum(jnp.sqrt(ss), 1e-12)
    k = k * (1.0 + (a - 1.0) * ka_ref[...])
    r_o[...] = r
    w_o[...] = decay
    k_o[...] = k
    v_o[...] = v
    a_o[...] = -kk
    b_o[...] = kk * a
    g_o[...] = g


def _rwkv_proj(x2d, seq, mix, w_rkv, w0, w1, w2, a0, a1, a2, g1, g2, k_k, k_a, p, pt, vres, ts=256):
    t, d = x2d.shape
    row = lambda v: v.reshape(1, d)
    tok = pl.BlockSpec((ts, d), lambda i: (i, 0))
    ins = [x2d, x2d, mix, w_rkv[0].astype(BF16), w_rkv[1].astype(BF16), w_rkv[2].astype(BF16),
           row(w0), w1.astype(BF16), w2.astype(BF16), row(a0), a1.astype(BF16), a2.astype(BF16),
           g1.astype(BF16), g2.astype(BF16), row(k_k), row(k_a), p, pt]
    specs = [_prev_rows_spec(ts, d), tok] + [_const_spec(a.shape) for a in ins[2:]]
    if vres is not None:
        v_first, v0, v1, v2 = vres
        extra = [v_first, row(v0), v1.astype(BF16), v2.astype(BF16)]
        ins += extra
        specs += [tok] + [_const_spec(a.shape) for a in extra[1:]]
    out_shape = [jax.ShapeDtypeStruct((t, d), F32)] * 7
    return pl.pallas_call(
        functools.partial(_rwkv_proj_kernel, ts=ts, seq=seq, has_vres=vres is not None),
        grid=(t // ts,),
        in_specs=specs,
        out_specs=[tok] * 7,
        out_shape=out_shape,
        compiler_params=pltpu.CompilerParams(dimension_semantics=("parallel",),
                                             vmem_limit_bytes=VMEM_LIMIT),
        name="rwkv_proj",
    )(*ins)


def _to_scan_kernel(x_ref, o_ref, zt_scr, *, gb, d, n):
    for b in range(gb):
        zt_scr[b * d:(b + 1) * d, :] = x_ref[b].T
    pairs = gb * d // n
    for c in range(n):
        o_ref[:, c, :] = zt_scr[pl.ds(c, pairs, stride=n), :].T


def _to_scan(x2d, gb, seq, tt=128):
    t, d = x2d.shape
    n = N_HEAD
    pairs = gb * d // n
    return pl.pallas_call(
        functools.partial(_to_scan_kernel, gb=gb, d=d, n=n),
        grid=(seq // tt,),
        in_specs=[pl.BlockSpec((gb, tt, d), lambda i: (0, i, 0))],
        out_specs=pl.BlockSpec((tt, n, pairs), lambda i: (i, 0, 0)),
        out_shape=jax.ShapeDtypeStruct((seq, n, pairs), F32),
        scratch_shapes=[pltpu.VMEM((gb * d, tt), F32)],
        compiler_params=pltpu.CompilerParams(dimension_semantics=("parallel",), vmem_limit_bytes=VMEM_LIMIT),
        name="to_scan_layout",
    )(x2d.reshape(gb, seq, d))


def _from_scan_kernel(y_ref, o_ref, zt_scr, *, gb, d, n):
    pairs = gb * d // n
    for c in range(n):
        zt_scr[pl.ds(c, pairs, stride=n), :] = y_ref[:, c, :].T
    for b in range(gb):
        o_ref[b] = zt_scr[b * d:(b + 1) * d, :].T


def _from_scan(y, gb, d, tt=128):
    seq, n, pairs = y.shape
    return pl.pallas_call(
        functools.partial(_from_scan_kernel, gb=gb, d=d, n=n),
        grid=(seq // tt,),
        in_specs=[pl.BlockSpec((tt, n, pairs), lambda i: (i, 0, 0))],
        out_specs=pl.BlockSpec((gb, tt, d), lambda i: (0, i, 0)),
        out_shape=jax.ShapeDtypeStruct((gb, seq, d), F32),
        scratch_shapes=[pltpu.VMEM((gb * d, tt), F32)],
        compiler_params=pltpu.CompilerParams(dimension_semantics=("parallel",), vmem_limit_bytes=VMEM_LIMIT),
        name="from_scan_layout",
    )(y).reshape(gb * seq, d)


def _wkv_kernel(r_ref, w_ref, k_ref, v_ref, a_ref, b_ref, y_ref, s_scr, *, tc, n):
    @pl.when(pl.program_id(1) == 0)
    def _():
        s_scr[...] = jnp.zeros_like(s_scr)

    def step(t, carry):
        a = a_ref[t]
        w = w_ref[t]
        b = b_ref[t]
        k = k_ref[t]
        r = r_ref[t]
        wr = w * r
        br = jnp.sum(b * r, axis=0, keepdims=True)
        kr = jnp.sum(k * r, axis=0, keepdims=True)

        def vloop(vi, c):
            s = s_scr[vi]
            sa = jnp.sum(s * a, axis=0, keepdims=True)
            y0 = jnp.sum(s * wr, axis=0, keepdims=True)
            vv = v_ref[t, pl.ds(vi, 1), :]
            s_scr[vi] = s * w + sa * b + vv * k
            y_ref[t, pl.ds(vi, 1), :] = y0 + sa * br + vv * kr
            return c

        return lax.fori_loop(0, n, vloop, carry, unroll=WKV_UNROLL)

    lax.fori_loop(0, tc, step, 0)


def _wkv(r, w, k, v, a, b, tc=16):
    s, n, pairs = r.shape
    spec = pl.BlockSpec((tc, n, LANES), lambda p, c: (c, 0, p))
    return pl.pallas_call(
        functools.partial(_wkv_kernel, tc=tc, n=n),
        grid=(pairs // LANES, s // tc),
        in_specs=[spec] * 6,
        out_specs=spec,
        out_shape=jax.ShapeDtypeStruct((s, n, pairs), F32),
        scratch_shapes=[pltpu.VMEM((n, n, LANES), F32)],
        compiler_params=pltpu.CompilerParams(dimension_semantics=("parallel", "arbitrary"),
                                             vmem_limit_bytes=VMEM_LIMIT),
        name="wkv7_scan",
    )(r, w, k, v, a, b)


def _rwkv_out_kernel(y_ref, r_ref, k_ref, v_ref, g_ref, x_ref, rk_ref, lxg_ref, lxb_ref, wo_ref,
                     lng_ref, lnb_ref, p_ref, pt_ref, o_ref, *, alpha):
    p = p_ref[...]
    pt = pt_ref[...]
    y = y_ref[...]
    inv_n = 1.0 / N_HEAD
    mu = _head_sum(y, p, pt) * inv_n
    yc = y - mu
    var = _head_sum(yc * yc, p, pt) * inv_n
    yn = yc * lax.rsqrt(var + GN_EPS) * lxg_ref[...] + lxb_ref[...]
    v = v_ref[...]
    bonus = _head_sum(r_ref[...] * k_ref[...] * rk_ref[...], p, pt) * v
    h = _bdot((yn + bonus) * g_ref[...], wo_ref[...])
    o_ref[...] = _layer_norm(alpha * x_ref[...] + h, lng_ref[...], lnb_ref[...])


def _rwkv_out(y, r, k, v, g, x2d, r_k, lnx_g, lnx_b, w_o, ln_g, ln_b, p, pt, alpha, ts=512):
    t, d = x2d.shape
    row = lambda a: a.reshape(1, d)
    tok = pl.BlockSpec((ts, d), lambda i: (i, 0))
    consts = [row(r_k), row(lnx_g), row(lnx_b), w_o.astype(BF16), row(ln_g), row(ln_b), p, pt]
    return pl.pallas_call(
        functools.partial(_rwkv_out_kernel, alpha=alpha),
        grid=(t // ts,),
        in_specs=[tok] * 6 + [_const_spec(a.shape) for a in consts],
        out_specs=tok,
        out_shape=jax.ShapeDtypeStruct((t, d), F32),
        compiler_params=pltpu.CompilerParams(dimension_semantics=("parallel",),
                                             vmem_limit_bytes=VMEM_LIMIT),
        name="rwkv_out",
    )(y, r, k, v, g, x2d, *consts)


def _ffn_kernel(xp_ref, x_ref, wup_ref, cw_ref, cb_ref, wdn_ref, lng_ref, lnb_ref, o_ref, u_scr,
                *, ts, seq, d_ff, fc, alpha):
    first = (pl.program_id(0) * ts) % seq == 0
    x = x_ref[...]
    xp = jnp.where(first, 0.0, xp_ref[...])
    xe = jnp.concatenate([xp, x], axis=0).astype(BF16)

    def conv_cols(off):
        u_scr[...] = jnp.dot(xe, wup_ref[:, off:off + fc], preferred_element_type=F32)
        cw = cw_ref[:, off:off + fc]
        return (u_scr[SUBLANES - 2:SUBLANES - 2 + ts, :] * cw[0:1]
                + u_scr[SUBLANES - 1:SUBLANES - 1 + ts, :] * cw[1:2]
                + u_scr[SUBLANES:SUBLANES + ts, :] * cw[2:3]
                + cb_ref[:, off:off + fc])

    acc = jnp.zeros(x.shape, F32)
    for c in range(d_ff // fc):
        gate = conv_cols(c * fc)
        val = conv_cols(d_ff + c * fc)
        act = jax.nn.silu(gate) * val
        acc = acc + jnp.dot(act.astype(BF16), wdn_ref[c * fc:(c + 1) * fc, :], preferred_element_type=F32)
    o_ref[...] = _layer_norm(alpha * x + acc, lng_ref[...], lnb_ref[...])


def _ffn(x2d, seq, w_up, conv_w, conv_b, w_down, ln_g, ln_b, alpha, ts=512, fc=1408):
    t, d = x2d.shape
    d_ff = w_down.shape[0]
    tok = pl.BlockSpec((ts, d), lambda i: (i, 0))
    single = pl.Buffered(1)
    wup_spec = pl.BlockSpec((d, 2 * d_ff), lambda i: (0, 0), pipeline_mode=single)
    wdn_spec = pl.BlockSpec((d_ff, d), lambda i: (0, 0), pipeline_mode=single)
    consts = [conv_w, conv_b.reshape(1, -1)]
    rows = [ln_g.reshape(1, d), ln_b.reshape(1, d)]
    return pl.pallas_call(
        functools.partial(_ffn_kernel, ts=ts, seq=seq, d_ff=d_ff, fc=fc, alpha=alpha),
        grid=(t // ts,),
        in_specs=[_prev_rows_spec(ts, d), tok, wup_spec] + [_const_spec(a.shape) for a in consts]
                 + [wdn_spec] + [_const_spec(a.shape) for a in rows],
        out_specs=tok,
        out_shape=jax.ShapeDtypeStruct((t, d), F32),
        scratch_shapes=[pltpu.VMEM((ts + SUBLANES, fc), F32)],
        compiler_params=pltpu.CompilerParams(dimension_semantics=("parallel",),
                                             vmem_limit_bytes=VMEM_LIMIT),
        name="conv_ffn",
    )(x2d, x2d, w_up.astype(BF16), *consts, w_down.astype(BF16), *rows)


def _dsa_proj_kernel(x_ref, wcq_ref, wckv_ref, wki_ref, wwit_ref, qg_ref, kvg_ref, wuq_ref, wukp_ref,
                     wqi_ref, kig_ref, kib_ref, qa_o, qi_o, ki_o, wit_o, ckv_o, ckvt_o, *, cw):
    x = x_ref[...].astype(BF16)
    c_q = jnp.dot(x, wcq_ref[...], preferred_element_type=F32)
    c_kv = jnp.dot(x, wckv_ref[...], preferred_element_type=F32)
    k_idx = jnp.dot(x, wki_ref[...], preferred_element_type=F32)
    w_idx_t = _bdot_nt(wwit_ref[...], x)

    def rms(z, g):
        return z * lax.rsqrt(jnp.mean(z * z, axis=-1, keepdims=True) + 1e-6) * g

    c_q = rms(c_q, qg_ref[...]).astype(BF16)
    c_kv = rms(c_kv, kvg_ref[...])
    ckv_o[...] = c_kv.astype(BF16)
    for jc in range(c_kv.shape[0] // cw):
        ckvt_o[jc] = c_kv[jc * cw:(jc + 1) * cw].T.astype(BF16)
    q = jnp.dot(c_q, wuq_ref[...], preferred_element_type=F32)
    qk_scale = QK_HEAD ** -0.5
    for hp in range(ATT_HEADS // 2):
        qa = _bdot(q[:, hp * LANES:(hp + 1) * LANES], wukp_ref[hp]) * qk_scale
        qa_o[2 * hp] = qa[:, :KV_LORA].astype(BF16)
        qa_o[2 * hp + 1] = qa[:, KV_LORA:].astype(BF16)
    for h in range(IDX_HEADS):
        qi_o[h] = jnp.dot(c_q, wqi_ref[h], preferred_element_type=F32).astype(BF16)
    ki_o[...] = _layer_norm(k_idx, kig_ref[...], kib_ref[...]).astype(BF16)
    wit_o[...] = w_idx_t * (IDX_HEADS ** -0.5 * IDX_DIM ** -0.5)


def _dsa_proj(x2d, w_in, q_norm_g, kv_norm_g, w_uq, w_uk, w_qidx, kidx_g, kidx_b, ts=512):
    t, d = x2d.shape
    cw = DSA_ROWS
    o1, o2, o3 = Q_LORA, Q_LORA + KV_LORA, Q_LORA + KV_LORA + IDX_DIM
    w_in = w_in.astype(BF16)
    z = jnp.zeros((ATT_HEADS // 2, QK_HEAD, KV_LORA), F32)
    wuk_pair = jnp.concatenate([jnp.concatenate([w_uk[0::2], z], axis=2),
                                jnp.concatenate([z, w_uk[1::2]], axis=2)], axis=1).astype(BF16)
    wqi = w_qidx.reshape(Q_LORA, IDX_HEADS, IDX_DIM).transpose(1, 0, 2).astype(BF16)
    consts = [w_in[:, :o1], w_in[:, o1:o2], w_in[:, o2:o3], w_in[:, o3:].T, q_norm_g.reshape(1, -1),
              kv_norm_g.reshape(1, -1), w_uq.astype(BF16), wuk_pair, wqi, kidx_g.reshape(1, -1),
              kidx_b.reshape(1, -1)]
    out_shape = [jax.ShapeDtypeStruct((ATT_HEADS, t, KV_LORA), BF16),
                 jax.ShapeDtypeStruct((IDX_HEADS, t, IDX_DIM), BF16),
                 jax.ShapeDtypeStruct((t, IDX_DIM), BF16),
                 jax.ShapeDtypeStruct((IDX_HEADS, t), F32),
                 jax.ShapeDtypeStruct((t, KV_LORA), BF16),
                 jax.ShapeDtypeStruct((t // cw, KV_LORA, cw), BF16)]
    out_specs = [pl.BlockSpec((ATT_HEADS, ts, KV_LORA), lambda i: (0, i, 0)),
                 pl.BlockSpec((IDX_HEADS, ts, IDX_DIM), lambda i: (0, i, 0)),
                 pl.BlockSpec((ts, IDX_DIM), lambda i: (i, 0)),
                 pl.BlockSpec((IDX_HEADS, ts), lambda i: (0, i)),
                 pl.BlockSpec((ts, KV_LORA), lambda i: (i, 0)),
                 pl.BlockSpec((ts // cw, KV_LORA, cw), lambda i: (i, 0, 0))]
    return pl.pallas_call(
        functools.partial(_dsa_proj_kernel, cw=cw),
        grid=(t // ts,),
        in_specs=[pl.BlockSpec((ts, d), lambda i: (i, 0))] + [_const_spec(a.shape) for a in consts],
        out_specs=out_specs,
        out_shape=out_shape,
        compiler_params=pltpu.CompilerParams(dimension_semantics=("parallel",),
                                             vmem_limit_bytes=VMEM_LIMIT),
        name="dsa_proj",
    )(x2d, *consts)


def _bias_tiles_kernel(rb_ref, o_ref, *, qb):
    c = lax.broadcasted_iota(jnp.int32, (2 * qb, qb), 0)
    tl = lax.broadcasted_iota(jnp.int32, (2 * qb, qb), 1)
    n = jnp.maximum(qb + tl - c, 0)
    max_exact = REL_BUCKETS // 2
    nf = jnp.maximum(n, 1).astype(F32)
    large = max_exact + (jnp.log(nf / max_exact) / math.log(REL_MAX_DIST / max_exact)
                         * (REL_BUCKETS - max_exact)).astype(jnp.int32)
    large = jnp.minimum(large, REL_BUCKETS - 1)
    bucket = jnp.where(n < max_exact, n, large)
    for h in range(ATT_HEADS):
        acc = jnp.zeros((2 * qb, qb), F32)
        for bkt in range(REL_BUCKETS):
            acc = jnp.where(bucket == bkt, rb_ref[bkt, h], acc)
        o_ref[h] = acc - rb_ref[REL_BUCKETS - 1, h]


def _bias_tiles(rel_bias, qb):
    return pl.pallas_call(
        functools.partial(_bias_tiles_kernel, qb=qb),
        in_specs=[pl.BlockSpec(memory_space=pltpu.SMEM)],
        out_specs=pl.BlockSpec(memory_space=pltpu.VMEM),
        out_shape=jax.ShapeDtypeStruct((ATT_HEADS, 2 * qb, qb), F32),
        name="rel_bias_tiles",
    )(rel_bias)


def _dsa_attn_kernel(qi_ref, wit_ref, ki_ref, ckv_ref, ckvt_ref, qa_ref, near_ref, tril_ref, wuvt_ref,
                     wo_ref, x_ref, lng_ref, lnb_ref, o_ref, key_scr, madd_scr, m_scr, den_scr, acc_scr,
                     *, topk, qb, alpha):
    cw = qb
    i = pl.program_id(1)
    nch = i + 1
    t_pos = i * qb + lax.broadcasted_iota(jnp.int32, (1, qb), 1)
    s_loc = lax.broadcasted_iota(jnp.int32, (cw, 1), 0)
    int_min = jnp.int32(INT_MIN)

    qi_all = qi_ref[...].reshape(IDX_HEADS * qb, IDX_DIM)
    wit = wit_ref[...]

    def score_chunk(c, carry):
        off = pl.multiple_of(c * cw, cw)
        s_all = _bdot_nt(ki_ref[pl.ds(off, cw), :], qi_all)
        score = jnp.zeros((cw, qb), F32)
        for h in range(IDX_HEADS):
            score = score + jnp.maximum(s_all[:, h * qb:(h + 1) * qb], 0.0) * wit[h:h + 1]
        bits = lax.bitcast_convert_type(score, jnp.int32)
        key = jnp.where(bits < 0, bits ^ jnp.int32(0x7FFFFFFF), bits)
        key_scr[c] = jnp.where(off + s_loc <= t_pos, key, int_min)
        return carry

    lax.fori_loop(0, nch, score_chunk, 0)

    def count(pred):
        def body(c, acc):
            kc = key_scr[c]
            for r in range(cw // SUBLANES):
                acc = acc + jnp.where(pred(kc[r * SUBLANES:(r + 1) * SUBLANES]), 1.0, 0.0)
            return acc

        acc = lax.fori_loop(0, nch, body, jnp.zeros((SUBLANES, qb), F32))
        return jnp.sum(acc, axis=0, keepdims=True)

    kf = float(topk)
    tau = jnp.where(count(lambda kc: kc >= 0) >= kf, jnp.int32(0), int_min)

    def bit_step(bi, tau):
        cand = tau | lax.shift_left(jnp.int32(1), jnp.int32(30) - bi)
        return jnp.where(count(lambda kc: kc >= cand) >= kf, cand, tau)

    tau = lax.fori_loop(0, 31, bit_step, tau)

    need = kf - count(lambda kc: kc > tau)
    has_kth = tau > int_min
    tril = tril_ref[...]

    def select_chunk(c, run):
        kc = key_scr[c]
        eq = jnp.logical_and(kc == tau, has_kth)
        e = jnp.where(eq, 1.0, 0.0)
        rank = jnp.dot(tril, e.astype(BF16), preferred_element_type=F32) + run
        sel = jnp.logical_or(kc > tau, jnp.logical_and(eq, rank <= need))
        madd_scr[c] = jnp.where(sel, 0.0, NEG_MASK)
        return run + jnp.sum(e, axis=0, keepdims=True)

    lax.fori_loop(0, nch, select_chunk, jnp.zeros((1, qb), F32))

    m_scr[...] = jnp.full(m_scr.shape, -jnp.inf, F32)
    den_scr[...] = jnp.zeros(den_scr.shape, F32)
    acc_scr[...] = jnp.zeros(acc_scr.shape, F32)

    def att_chunk(c, with_bias):
        off = pl.multiple_of(c * cw, cw)
        ckv_c = ckv_ref[pl.ds(off, cw), :]
        ckvt_c = ckvt_ref[c]
        boff = pl.multiple_of(jnp.where(c == i, qb, 0), qb)

        def head(h, carry):
            l = _bdot_nt(ckv_c, qa_ref[h]) + madd_scr[c]
            if with_bias:
                l = l + near_ref[h, pl.ds(boff, qb), :]
            m_old = m_scr[h]
            m_new = jnp.maximum(m_old, jnp.max(l, axis=0, keepdims=True))
            p = jnp.exp(l - m_new)
            scale = jnp.exp(m_old - m_new)
            den_scr[h] = den_scr[h] * scale + jnp.sum(p, axis=0, keepdims=True)
            m_scr[h] = m_new
            acc_scr[h] = acc_scr[h] * scale + jnp.dot(ckvt_c, p.astype(BF16), preferred_element_type=F32)
            return carry

        for h in range(ATT_HEADS):
            head(h, 0)

    def far_body(c, carry):
        att_chunk(c, False)
        return carry

    def near_body(c, carry):
        att_chunk(c, True)
        return carry

    c_near = jnp.maximum(i - 1, 0)
    lax.fori_loop(0, c_near, far_body, 0)
    lax.fori_loop(c_near, nch, near_body, 0)

    o_rows = []
    for h in range(ATT_HEADS):
        o_lat_t = acc_scr[h] / den_scr[h]
        o_rows.append(_bdot(wuvt_ref[h], o_lat_t))
    h_out = _bdot(jnp.concatenate(o_rows, axis=0).T, wo_ref[...])
    o_ref[...] = _layer_norm(alpha * x_ref[...] + h_out, lng_ref[...], lnb_ref[...])


def _dsa_attn(x2d, batch, seq, qa, qi, ki, wit, ckv, ckvt, near, w_uv, w_o, ln_g, ln_b, alpha):
    t, d = x2d.shape
    qb = DSA_ROWS
    assert KV_LORA == LANES
    nb = seq // qb
    topk = min(TOPK_MAX, seq // 4)
    tril = (jnp.arange(qb)[:, None] >= jnp.arange(qb)[None, :]).astype(BF16)
    wuv_t = w_uv.transpose(0, 2, 1).astype(BF16)
    wo = w_o.astype(BF16)
    tok = lambda b, i: (b * nb + i, 0)
    in_specs = [
        pl.BlockSpec((IDX_HEADS, qb, IDX_DIM), lambda b, i: (0, b * nb + i, 0)),
        pl.BlockSpec((IDX_HEADS, qb), lambda b, i: (0, b * nb + i)),
        pl.BlockSpec((seq, IDX_DIM), lambda b, i: (b, 0)),
        pl.BlockSpec((seq, KV_LORA), lambda b, i: (b, 0)),
        pl.BlockSpec((nb, KV_LORA, qb), lambda b, i: (b, 0, 0)),
        pl.BlockSpec((ATT_HEADS, qb, KV_LORA), lambda b, i: (0, b * nb + i, 0)),
        pl.BlockSpec(near.shape, lambda b, i: (0, 0, 0), pipeline_mode=pl.Buffered(1)),
        _const_spec(tril.shape),
        _const_spec(wuv_t.shape),
        _const_spec(wo.shape),
        pl.BlockSpec((qb, d), tok),
        _const_spec((1, d)),
        _const_spec((1, d)),
    ]
    scratch = [pltpu.VMEM((nb, qb, qb), jnp.int32),
               pltpu.VMEM((nb, qb, qb), F32),
               pltpu.VMEM((ATT_HEADS, 1, qb), F32),
               pltpu.VMEM((ATT_HEADS, 1, qb), F32),
               pltpu.VMEM((ATT_HEADS, KV_LORA, qb), F32)]
    return pl.pallas_call(
        functools.partial(_dsa_attn_kernel, topk=topk, qb=qb, alpha=alpha),
        grid=(batch, nb),
        in_specs=in_specs,
        out_specs=pl.BlockSpec((qb, d), tok),
        out_shape=jax.ShapeDtypeStruct((t, d), F32),
        scratch_shapes=scratch,
        compiler_params=pltpu.CompilerParams(dimension_semantics=("parallel", "parallel"),
                                             vmem_limit_bytes=VMEM_LIMIT),
        name="dsa_attn",
    )(qi, wit, ki, ckv, ckvt, qa, near, tril, wuv_t, wo, x2d, ln_g.reshape(1, d), ln_b.reshape(1, d))


def kernel(x, ln_g, ln_b, rwkv_mix, rwkv_w_rkv, rwkv_w0, rwkv_w1, rwkv_w2, rwkv_a0, rwkv_a1, rwkv_a2, rwkv_v0, rwkv_v1, rwkv_v2, rwkv_g1, rwkv_g2, rwkv_k_k, rwkv_k_a, rwkv_r_k, rwkv_lnx_g, rwkv_lnx_b, rwkv_w_o, dsa_w_in, dsa_q_norm_g, dsa_kv_norm_g, dsa_w_uq, dsa_w_uk, dsa_w_uv, dsa_w_qidx, dsa_kidx_g, dsa_kidx_b, dsa_w_o, rel_bias, ffn_w_up, ffn_conv_w, ffn_conv_b, ffn_w_down):
    batch, seq, d = x.shape
    depth = ln_g.shape[0]
    heads = d // N_HEAD
    alpha = (2 * depth) ** 0.25
    gb = LANES // heads
    groups = batch // gb
    t = gb * seq
    xs = [x[g * gb:(g + 1) * gb].reshape(t, d) for g in range(groups)]

    p = (jnp.arange(d)[:, None] // N_HEAD == jnp.arange(LANES)[None, :]).astype(BF16)
    pt = p.T
    near = _bias_tiles(rel_bias, DSA_ROWS)

    to_scan = functools.partial(_to_scan, gb=gb, seq=seq)

    v_first = [None] * groups
    for i in range(depth):
        j = i // 2
        for g in range(groups):
            x2d = xs[g]
            if i % 2 == 0:
                vres = None if j == 0 else (v_first[g], rwkv_v0[j - 1], rwkv_v1[j - 1], rwkv_v2[j - 1])
                r, w, k, v, a_s, b_s, gate = _rwkv_proj(
                    x2d, seq, rwkv_mix[j], rwkv_w_rkv[j], rwkv_w0[j], rwkv_w1[j], rwkv_w2[j], rwkv_a0[j],
                    rwkv_a1[j], rwkv_a2[j], rwkv_g1[j], rwkv_g2[j], rwkv_k_k[j], rwkv_k_a[j], p, pt, vres)
                if j == 0:
                    v_first[g] = v
                y = _wkv(to_scan(r), to_scan(w), to_scan(k), to_scan(v), to_scan(a_s), to_scan(b_s))
                x2d = _rwkv_out(_from_scan(y, gb, d), r, k, v, gate, x2d, rwkv_r_k[j], rwkv_lnx_g[j], rwkv_lnx_b[j],
                                rwkv_w_o[j], ln_g[i, 0], ln_b[i, 0], p, pt, alpha)
            else:
                qa, qi, ki, wit, ckv, ckvt = _dsa_proj(x2d, dsa_w_in[j], dsa_q_norm_g[j], dsa_kv_norm_g[j],
                                                       dsa_w_uq[j], dsa_w_uk[j], dsa_w_qidx[j], dsa_kidx_g[j],
                                                       dsa_kidx_b[j])
                x2d = _dsa_attn(x2d, gb, seq, qa, qi, ki, wit, ckv, ckvt, near, dsa_w_uv[j], dsa_w_o[j],
                                ln_g[i, 0], ln_b[i, 0], alpha)
            xs[g] = _ffn(x2d, seq, ffn_w_up[i], ffn_conv_w[i], ffn_conv_b[i], ffn_w_down[i],
                         ln_g[i, 1], ln_b[i, 1], alpha)
    return jnp.concatenate([z.reshape(gb, seq, d) for z in xs], axis=0)
```

```python
import functools
import math

import jax
import jax.numpy as jnp
from jax import lax
from jax.experimental import pallas as pl
from jax.experimental.pallas import tpu as pltpu

F32 = jnp.float32
BF16 = jnp.bfloat16

N_HEAD = 64
ATT_HEADS = 16
QK_HEAD = 64
V_HEAD = 64
Q_LORA = 256
KV_LORA = 128
IDX_HEADS = 8
IDX_DIM = 64
TOPK_MAX = 256
DSA_ROWS = 256
REL_BUCKETS = 32
REL_MAX_DIST = 128
CONV_W = 3
GN_EPS = 64e-5
LN_EPS = 1e-5

LANES = 128
SUBLANES = 8
VMEM_LIMIT = 56 * 1024 * 1024
WKV_UNROLL = 2

NEG_MASK = -1e30
INT_MIN = -2 ** 31


def _bdot(a, b):
    return jnp.dot(a.astype(BF16), b.astype(BF16), preferred_element_type=F32)


def _bdot_nt(a, b):
    return lax.dot_general(a.astype(BF16), b.astype(BF16), (((1,), (1,)), ((), ())),
                           preferred_element_type=F32)


def _split_dot(a, b):
    hi = a.astype(BF16)
    lo = (a - hi.astype(F32)).astype(BF16)
    return (jnp.dot(hi, b, preferred_element_type=F32) + jnp.dot(lo, b, preferred_element_type=F32))


def _layer_norm(x, g, b):
    mu = jnp.mean(x, axis=-1, keepdims=True)
    xc = x - mu
    var = jnp.mean(xc * xc, axis=-1, keepdims=True)
    return xc * lax.rsqrt(var + LN_EPS) * g + b


def _head_sum(x, p, pt):
    return _split_dot(_split_dot(x, p), pt)


def _const_spec(shape):
    nd = len(shape)
    return pl.BlockSpec(shape, lambda *_: (0,) * nd)


def _prev_rows_spec(ts, d):
    return pl.BlockSpec((SUBLANES, d), lambda i: (jnp.maximum(i * (ts // SUBLANES) - 1, 0), 0))


def _rwkv_proj_kernel(*refs, ts, seq, has_vres):
    (xp_ref, x_ref, mix_ref, wr_ref, wk_ref, wv_ref, w0_ref, w1_ref, w2_ref, a0_ref, a1_ref,
     a2_ref, g1_ref, g2_ref, kk_ref, ka_ref, p_ref, pt_ref) = refs[:18]
    if has_vres:
        vf_ref, v0_ref, v1_ref, v2_ref = refs[18:22]
        outs = refs[22:]
    else:
        outs = refs[18:]
    r_o, w_o, k_o, v_o, a_o, b_o, g_o = outs

    first = (pl.program_id(0) * ts) % seq == 0
    x = x_ref[...]
    p_last = jnp.where(first, 0.0, xp_ref[SUBLANES - 1:SUBLANES, :])
    rows = lax.broadcasted_iota(jnp.int32, (ts, 1), 0)
    x_shift = jnp.where(rows == 0, p_last, pltpu.roll(x, 1, 0))
    xx = x_shift - x
    mix = mix_ref[...]

    def xm(j):
        return x + xx * mix[j:j + 1]

    r = _bdot(xm(0), wr_ref[...])
    k = _bdot(xm(1), wk_ref[...])
    xv = xm(2)
    v = _bdot(xv, wv_ref[...])
    w_log = -jax.nn.softplus(-(w0_ref[...] + _bdot(jnp.tanh(_bdot(xm(3), w1_ref[...])), w2_ref[...]))) - 0.5
    decay = jnp.exp(-jnp.exp(w_log))
    if has_vres:
        v = v + (vf_ref[...] - v) * jax.nn.sigmoid(v0_ref[...] + _bdot(_bdot(xv, v1_ref[...]), v2_ref[...]))
    a = jax.nn.sigmoid(a0_ref[...] + _bdot(_bdot(xm(4), a1_ref[...]), a2_ref[...]))
    g = _bdot(jax.nn.sigmoid(_bdot(xm(5), g1_ref[...])), g2_ref[...])
    kk = k * kk_ref[...]
    ss = _head_sum(kk * kk, p_ref[...], pt_ref[...])
    kk = kk / jnp.maximum(jnp.sqrt(ss), 1e-12)
    k = k * (1.0 + (a - 1.0) * ka_ref[...])
    r_o[...] = r
    w_o[...] = decay
    k_o[...] = k
    v_o[...] = v
    a_o[...] = -kk
    b_o[...] = kk * a
    g_o[...] = g


def _rwkv_proj(x2d, seq, mix, w_rkv, w0, w1, w2, a0, a1, a2, g1, g2, k_k, k_a, p, pt, vres, ts=256):
    t, d = x2d.shape
    row = lambda v: v.reshape(1, d)
    tok = pl.BlockSpec((ts, d), lambda i: (i, 0))
    ins = [x2d, x2d, mix, w_rkv[0].astype(BF16), w_rkv[1].astype(BF16), w_rkv[2].astype(BF16),
           row(w0), w1.astype(BF16), w2.astype(BF16), row(a0), a1.astype(BF16), a2.astype(BF16),
           g1.astype(BF16), g2.astype(BF16), row(k_k), row(k_a), p, pt]
    specs = [_prev_rows_spec(ts, d), tok] + [_const_spec(a.shape) for a in ins[2:]]
    if vres is not None:
        v_first, v0, v1, v2 = vres
        extra = [v_first, row(v0), v1.astype(BF16), v2.astype(BF16)]
        ins += extra
        specs += [tok] + [_const_spec(a.shape) for a in extra[1:]]
    out_shape = [jax.ShapeDtypeStruct((t, d), F32)] * 7
    return pl.pallas_call(
        functools.partial(_rwkv_proj_kernel, ts=ts, seq=seq, has_vres=vres is not None),
        grid=(t // ts,),
        in_specs=specs,
        out_specs=[tok] * 7,
        out_shape=out_shape,
        compiler_params=pltpu.CompilerParams(dimension_semantics=("parallel",),
                                             vmem_limit_bytes=VMEM_LIMIT),
        name="rwkv_proj",
    )(*ins)


def _to_scan_kernel(x_ref, o_ref, zt_scr, *, gb, d, n):
    for b in range(gb):
        zt_scr[b * d:(b + 1) * d, :] = x_ref[b].T
    pairs = gb * d // n
    for c in range(n):
        o_ref[:, c, :] = zt_scr[pl.ds(c, pairs, stride=n), :].T


def _to_scan(x2d, gb, seq, tt=128):
    t, d = x2d.shape
    n = N_HEAD
    pairs = gb * d // n
    return pl.pallas_call(
        functools.partial(_to_scan_kernel, gb=gb, d=d, n=n),
        grid=(seq // tt,),
        in_specs=[pl.BlockSpec((gb, tt, d), lambda i: (0, i, 0))],
        out_specs=pl.BlockSpec((tt, n, pairs), lambda i: (i, 0, 0)),
        out_shape=jax.ShapeDtypeStruct((seq, n, pairs), F32),
        scratch_shapes=[pltpu.VMEM((gb * d, tt), F32)],
        compiler_params=pltpu.CompilerParams(dimension_semantics=("parallel",), vmem_limit_bytes=VMEM_LIMIT),
        name="to_scan_layout",
    )(x2d.reshape(gb, seq, d))


def _from_scan_kernel(y_ref, o_ref, zt_scr, *, gb, d, n):
    pairs = gb * d // n
    for c in range(n):
        zt_scr[pl.ds(c, pairs, stride=n), :] = y_ref[:, c, :].T
    for b in range(gb):
        o_ref[b] = zt_scr[b * d:(b + 1) * d, :].T


def _from_scan(y, gb, d, tt=128):
    seq, n, pairs = y.shape
    return pl.pallas_call(
        functools.partial(_from_scan_kernel, gb=gb, d=d, n=n),
        grid=(seq // tt,),
        in_specs=[pl.BlockSpec((tt, n, pairs), lambda i: (i, 0, 0))],
        out_specs=pl.BlockSpec((gb, tt, d), lambda i: (0, i, 0)),
        out_shape=jax.ShapeDtypeStruct((gb, seq, d), F32),
        scratch_shapes=[pltpu.VMEM((gb * d, tt), F32)],
        compiler_params=pltpu.CompilerParams(dimension_semantics=("parallel",), vmem_limit_bytes=VMEM_LIMIT),
        name="from_scan_layout",
    )(y).reshape(gb * seq, d)


def _rows_of_sums(parts):
    row = lax.broadcasted_iota(jnp.int32, parts[0].shape, 0)
    dist = SUBLANES // 2
    while dist >= 1:
        lower = (row & dist) == 0
        half = len(parts) // 2
        parts = [jnp.where(lower, parts[j] + pltpu.roll(parts[j], SUBLANES - dist, 0),
                           parts[j + half] + pltpu.roll(parts[j + half], dist, 0)) for j in range(half)]
        dist //= 2
    return parts[0]


def _wkv_kernel(r_ref, w_ref, k_ref, v_ref, a_ref, b_ref, y_ref, s_scr, *, tc, n):
    @pl.when(pl.program_id(1) == 0)
    def _():
        s_scr[...] = jnp.zeros_like(s_scr)

    groups = n // SUBLANES

    def step(t, carry):
        a = a_ref[t]
        w = w_ref[t]
        b = b_ref[t]
        k = k_ref[t]
        r = r_ref[t]
        wr = w * r
        br = jnp.sum(b * r, axis=0, keepdims=True)
        kr = jnp.sum(k * r, axis=0, keepdims=True)

        def vblock(vb, c):
            base = pl.multiple_of(vb * SUBLANES, SUBLANES)
            pa, py = [], []
            for j in range(SUBLANES):
                s = s_scr[base + j]
                pa.append(jnp.sum((s * a).reshape(groups, SUBLANES, LANES), axis=0))
                py.append(jnp.sum((s * wr).reshape(groups, SUBLANES, LANES), axis=0))
            sa = _rows_of_sums(pa)
            vv = v_ref[t, pl.ds(base, SUBLANES), :]
            y_ref[t, pl.ds(base, SUBLANES), :] = _rows_of_sums(py) + sa * br + vv * kr
            for j in range(SUBLANES):
                s_scr[base + j] = s_scr[base + j] * w + sa[j:j + 1] * b + vv[j:j + 1] * k
            return c

        return lax.fori_loop(0, groups, vblock, carry, unroll=WKV_UNROLL)

    lax.fori_loop(0, tc, step, 0)


def _wkv(r, w, k, v, a, b, tc=16):
    s, n, pairs = r.shape
    spec = pl.BlockSpec((tc, n, LANES), lambda p, c: (c, 0, p))
    return pl.pallas_call(
        functools.partial(_wkv_kernel, tc=tc, n=n),
        grid=(pairs // LANES, s // tc),
        in_specs=[spec] * 6,
        out_specs=spec,
        out_shape=jax.ShapeDtypeStruct((s, n, pairs), F32),
        scratch_shapes=[pltpu.VMEM((n, n, LANES), F32)],
        compiler_params=pltpu.CompilerParams(dimension_semantics=("parallel", "arbitrary"),
                                             vmem_limit_bytes=VMEM_LIMIT),
        name="wkv7_scan",
    )(r, w, k, v, a, b)


def _rwkv_out_kernel(y_ref, r_ref, k_ref, v_ref, g_ref, x_ref, rk_ref, lxg_ref, lxb_ref, wo_ref,
                     lng_ref, lnb_ref, p_ref, pt_ref, o_ref, *, alpha):
    p = p_ref[...]
    pt = pt_ref[...]
    y = y_ref[...]
    inv_n = 1.0 / N_HEAD
    mu = _head_sum(y, p, pt) * inv_n
    yc = y - mu
    var = _head_sum(yc * yc, p, pt) * inv_n
    yn = yc * lax.rsqrt(var + GN_EPS) * lxg_ref[...] + lxb_ref[...]
    v = v_ref[...]
    bonus = _head_sum(r_ref[...] * k_ref[...] * rk_ref[...], p, pt) * v
    h = _bdot((yn + bonus) * g_ref[...], wo_ref[...])
    o_ref[...] = _layer_norm(alpha * x_ref[...] + h, lng_ref[...], lnb_ref[...])


def _rwkv_out(y, r, k, v, g, x2d, r_k, lnx_g, lnx_b, w_o, ln_g, ln_b, p, pt, alpha, ts=512):
    t, d = x2d.shape
    row = lambda a: a.reshape(1, d)
    tok = pl.BlockSpec((ts, d), lambda i: (i, 0))
    consts = [row(r_k), row(lnx_g), row(lnx_b), w_o.astype(BF16), row(ln_g), row(ln_b), p, pt]
    return pl.pallas_call(
        functools.partial(_rwkv_out_kernel, alpha=alpha),
        grid=(t // ts,),
        in_specs=[tok] * 6 + [_const_spec(a.shape) for a in consts],
        out_specs=tok,
        out_shape=jax.ShapeDtypeStruct((t, d), F32),
        compiler_params=pltpu.CompilerParams(dimension_semantics=("parallel",),
                                             vmem_limit_bytes=VMEM_LIMIT),
        name="rwkv_out",
    )(y, r, k, v, g, x2d, *consts)


def _ffn_kernel(xp_ref, x_ref, wup_ref, cw_ref, cb_ref, wdn_ref, lng_ref, lnb_ref, o_ref, u_scr,
                *, ts, seq, d_ff, fc, alpha):
    first = (pl.program_id(0) * ts) % seq == 0
    x = x_ref[...]
    xp = jnp.where(first, 0.0, xp_ref[...])
    xe = jnp.concatenate([xp, x], axis=0).astype(BF16)

    def conv_cols(off):
        u_scr[...] = jnp.dot(xe, wup_ref[:, off:off + fc], preferred_element_type=F32)
        cw = cw_ref[:, off:off + fc]
        return (u_scr[SUBLANES - 2:SUBLANES - 2 + ts, :] * cw[0:1]
                + u_scr[SUBLANES - 1:SUBLANES - 1 + ts, :] * cw[1:2]
                + u_scr[SUBLANES:SUBLANES + ts, :] * cw[2:3]
                + cb_ref[:, off:off + fc])

    acc = jnp.zeros(x.shape, F32)
    for c in range(d_ff // fc):
        gate = conv_cols(c * fc)
        val = conv_cols(d_ff + c * fc)
        act = jax.nn.silu(gate) * val
        acc = acc + jnp.dot(act.astype(BF16), wdn_ref[c * fc:(c + 1) * fc, :], preferred_element_type=F32)
    o_ref[...] = _layer_norm(alpha * x + acc, lng_ref[...], lnb_ref[...])


def _ffn(x2d, seq, w_up, conv_w, conv_b, w_down, ln_g, ln_b, alpha, ts=512, fc=1408):
    t, d = x2d.shape
    d_ff = w_down.shape[0]
    tok = pl.BlockSpec((ts, d), lambda i: (i, 0))
    single = pl.Buffered(1)
    wup_spec = pl.BlockSpec((d, 2 * d_ff), lambda i: (0, 0), pipeline_mode=single)
    wdn_spec = pl.BlockSpec((d_ff, d), lambda i: (0, 0), pipeline_mode=single)
    consts = [conv_w, conv_b.reshape(1, -1)]
    rows = [ln_g.reshape(1, d), ln_b.reshape(1, d)]
    return pl.pallas_call(
        functools.partial(_ffn_kernel, ts=ts, seq=seq, d_ff=d_ff, fc=fc, alpha=alpha),
        grid=(t // ts,),
        in_specs=[_prev_rows_spec(ts, d), tok, wup_spec] + [_const_spec(a.shape) for a in consts]
                 + [wdn_spec] + [_const_spec(a.shape) for a in rows],
        out_specs=tok,
        out_shape=jax.ShapeDtypeStruct((t, d), F32),
        scratch_shapes=[pltpu.VMEM((ts + SUBLANES, fc), F32)],
        compiler_params=pltpu.CompilerParams(dimension_semantics=("parallel",),
                                             vmem_limit_bytes=VMEM_LIMIT),
        name="conv_ffn",
    )(x2d, x2d, w_up.astype(BF16), *consts, w_down.astype(BF16), *rows)


def _dsa_proj_kernel(x_ref, wcq_ref, wckv_ref, wki_ref, wwit_ref, qg_ref, kvg_ref, wuq_ref, wukp_ref,
                     wqi_ref, kig_ref, kib_ref, qa_o, qi_o, ki_o, wit_o, ckv_o, ckvt_o, *, cw):
    x = x_ref[...].astype(BF16)
    c_q = jnp.dot(x, wcq_ref[...], preferred_element_type=F32)
    c_kv = jnp.dot(x, wckv_ref[...], preferred_element_type=F32)
    k_idx = jnp.dot(x, wki_ref[...], preferred_element_type=F32)
    w_idx_t = _bdot_nt(wwit_ref[...], x)

    def rms(z, g):
        return z * lax.rsqrt(jnp.mean(z * z, axis=-1, keepdims=True) + 1e-6) * g

    c_q = rms(c_q, qg_ref[...]).astype(BF16)
    c_kv = rms(c_kv, kvg_ref[...])
    ckv_o[...] = c_kv.astype(BF16)
    for jc in range(c_kv.shape[0] // cw):
        ckvt_o[jc] = c_kv[jc * cw:(jc + 1) * cw].T.astype(BF16)
    q = jnp.dot(c_q, wuq_ref[...], preferred_element_type=F32)
    qk_scale = QK_HEAD ** -0.5
    for hp in range(ATT_HEADS // 2):
        qa = _bdot(q[:, hp * LANES:(hp + 1) * LANES], wukp_ref[hp]) * qk_scale
        qa_o[2 * hp] = qa[:, :KV_LORA].astype(BF16)
        qa_o[2 * hp + 1] = qa[:, KV_LORA:].astype(BF16)
    for h in range(IDX_HEADS):
        qi_o[h] = jnp.dot(c_q, wqi_ref[h], preferred_element_type=F32).astype(BF16)
    ki_o[...] = _layer_norm(k_idx, kig_ref[...], kib_ref[...]).astype(BF16)
    wit_o[...] = w_idx_t * (IDX_HEADS ** -0.5 * IDX_DIM ** -0.5)


def _dsa_proj(x2d, w_in, q_norm_g, kv_norm_g, w_uq, w_uk, w_qidx, kidx_g, kidx_b, ts=512):
    t, d = x2d.shape
    cw = DSA_ROWS
    o1, o2, o3 = Q_LORA, Q_LORA + KV_LORA, Q_LORA + KV_LORA + IDX_DIM
    w_in = w_in.astype(BF16)
    z = jnp.zeros((ATT_HEADS // 2, QK_HEAD, KV_LORA), F32)
    wuk_pair = jnp.concatenate([jnp.concatenate([w_uk[0::2], z], axis=2),
                                jnp.concatenate([z, w_uk[1::2]], axis=2)], axis=1).astype(BF16)
    wqi = w_qidx.reshape(Q_LORA, IDX_HEADS, IDX_DIM).transpose(1, 0, 2).astype(BF16)
    consts = [w_in[:, :o1], w_in[:, o1:o2], w_in[:, o2:o3], w_in[:, o3:].T, q_norm_g.reshape(1, -1),
              kv_norm_g.reshape(1, -1), w_uq.astype(BF16), wuk_pair, wqi, kidx_g.reshape(1, -1),
              kidx_b.reshape(1, -1)]
    out_shape = [jax.ShapeDtypeStruct((ATT_HEADS, t, KV_LORA), BF16),
                 jax.ShapeDtypeStruct((IDX_HEADS, t, IDX_DIM), BF16),
                 jax.ShapeDtypeStruct((t, IDX_DIM), BF16),
                 jax.ShapeDtypeStruct((IDX_HEADS, t), F32),
                 jax.ShapeDtypeStruct((t, KV_LORA), BF16),
                 jax.ShapeDtypeStruct((t // cw, KV_LORA, cw), BF16)]
    out_specs = [pl.BlockSpec((ATT_HEADS, ts, KV_LORA), lambda i: (0, i, 0)),
                 pl.BlockSpec((IDX_HEADS, ts, IDX_DIM), lambda i: (0, i, 0)),
                 pl.BlockSpec((ts, IDX_DIM), lambda i: (i, 0)),
                 pl.BlockSpec((IDX_HEADS, ts), lambda i: (0, i)),
                 pl.BlockSpec((ts, KV_LORA), lambda i: (i, 0)),
                 pl.BlockSpec((ts // cw, KV_LORA, cw), lambda i: (i, 0, 0))]
    return pl.pallas_call(
        functools.partial(_dsa_proj_kernel, cw=cw),
        grid=(t // ts,),
        in_specs=[pl.BlockSpec((ts, d), lambda i: (i, 0))] + [_const_spec(a.shape) for a in consts],
        out_specs=out_specs,
        out_shape=out_shape,
        compiler_params=pltpu.CompilerParams(dimension_semantics=("parallel",),
                                             vmem_limit_bytes=VMEM_LIMIT),
        name="dsa_proj",
    )(x2d, *consts)


def _bias_tiles_kernel(rb_ref, o_ref, *, qb):
    c = lax.broadcasted_iota(jnp.int32, (2 * qb, qb), 0)
    tl = lax.broadcasted_iota(jnp.int32, (2 * qb, qb), 1)
    n = jnp.maximum(qb + tl - c, 0)
    max_exact = REL_BUCKETS // 2
    nf = jnp.maximum(n, 1).astype(F32)
    large = max_exact + (jnp.log(nf / max_exact) / math.log(REL_MAX_DIST / max_exact)
                         * (REL_BUCKETS - max_exact)).astype(jnp.int32)
    large = jnp.minimum(large, REL_BUCKETS - 1)
    bucket = jnp.where(n < max_exact, n, large)
    for h in range(ATT_HEADS):
        acc = jnp.zeros((2 * qb, qb), F32)
        for bkt in range(REL_BUCKETS):
            acc = jnp.where(bucket == bkt, rb_ref[bkt, h], acc)
        o_ref[h] = acc - rb_ref[REL_BUCKETS - 1, h]


def _bias_tiles(rel_bias, qb):
    return pl.pallas_call(
        functools.partial(_bias_tiles_kernel, qb=qb),
        in_specs=[pl.BlockSpec(memory_space=pltpu.SMEM)],
        out_specs=pl.BlockSpec(memory_space=pltpu.VMEM),
        out_shape=jax.ShapeDtypeStruct((ATT_HEADS, 2 * qb, qb), F32),
        name="rel_bias_tiles",
    )(rel_bias)


def _dsa_attn_kernel(qi_ref, wit_ref, ki_ref, ckv_ref, ckvt_ref, qa_ref, near_ref, tril_ref, wuvt_ref,
                     wo_ref, x_ref, lng_ref, lnb_ref, o_ref, key_scr, khi_scr, klo_scr, madd_scr, m_scr, den_scr, acc_scr,
                     sc_scr, l_scr, p_scr,
                     *, topk, qb, alpha):
    cw = qb
    i = pl.program_id(1)
    nch = i + 1
    t_pos = i * qb + lax.broadcasted_iota(jnp.int32, (1, qb), 1)
    s_loc = lax.broadcasted_iota(jnp.int32, (cw, 1), 0)
    int_min = jnp.int32(INT_MIN)

    qi_all = qi_ref[...].reshape(IDX_HEADS * qb, IDX_DIM)
    wit = wit_ref[...]

    def score_chunk(c, carry):
        off = pl.multiple_of(c * cw, cw)
        s_all = _bdot_nt(ki_ref[pl.ds(off, cw), :], qi_all)
        score = jnp.zeros((cw, qb), F32)
        for h in range(IDX_HEADS):
            score = score + jnp.maximum(s_all[:, h * qb:(h + 1) * qb], 0.0) * wit[h:h + 1]
        bits = lax.bitcast_convert_type(score, jnp.int32)
        key = jnp.where(bits < 0, bits ^ jnp.int32(0x7FFFFFFF), bits)
        key = jnp.where(off + s_loc <= t_pos, key, int_min)
        key_scr[c] = key
        khi_scr[c] = lax.shift_right_arithmetic(key, 16).astype(jnp.int16)
        klo_scr[c] = ((key & 0xFFFF) - 32768).astype(jnp.int16)
        return carry

    lax.fori_loop(0, nch, score_chunk, 0)

    def count(pred):
        def body(c, acc):
            kc = key_scr[c]
            for r in range(cw // SUBLANES):
                acc = acc + jnp.where(pred(kc[r * SUBLANES:(r + 1) * SUBLANES]), 1.0, 0.0)
            return acc

        acc = lax.fori_loop(0, nch, body, jnp.zeros((SUBLANES, qb), F32))
        return jnp.sum(acc, axis=0, keepdims=True)

    rows16 = 2 * SUBLANES
    one16, zero16, min16 = jnp.int16(1), jnp.int16(0), jnp.int16(-32768)

    def count16(scr, pred):
        def body(c, acc):
            kc = scr[c]
            for r in range(cw // rows16):
                acc = acc + jnp.where(pred(kc[r * rows16:(r + 1) * rows16]), one16, zero16)
            return acc

        acc = lax.fori_loop(0, nch, body, jnp.zeros((rows16, qb), jnp.int16))
        return jnp.sum(acc.astype(jnp.int32), axis=0, keepdims=True)

    def search16(scr, want):
        tau = jnp.where(count16(scr, lambda x: x >= zero16) >= want, jnp.int32(0), jnp.int32(-32768))

        def bit_step(bi, tau):
            cand = tau | lax.shift_left(jnp.int32(1), jnp.int32(14) - bi)
            c16 = cand.astype(jnp.int16)
            return jnp.where(count16(scr, lambda x: x >= c16) >= want, cand, tau)

        return lax.fori_loop(0, 15, bit_step, tau)

    tau_hi = search16(khi_scr, jnp.int32(topk))
    th16 = tau_hi.astype(jnp.int16)
    want_lo = topk - count16(khi_scr, lambda x: x > th16)

    def mask_chunk(c, carry):
        klo_scr[c] = jnp.where(khi_scr[c] == th16, klo_scr[c], min16)
        return carry

    lax.fori_loop(0, nch, mask_chunk, 0)
    tau_lo = search16(klo_scr, want_lo)
    tau = lax.shift_left(tau_hi, 16) | ((tau_lo + 32768) & 0xFFFF)
    kf = float(topk)

    need = kf - count(lambda kc: kc > tau)
    has_kth = tau > int_min
    tril = tril_ref[...]

    def select_chunk(c, run):
        kc = key_scr[c]
        eq = jnp.logical_and(kc == tau, has_kth)
        e = jnp.where(eq, 1.0, 0.0)
        rank = jnp.dot(tril, e.astype(BF16), preferred_element_type=F32) + run
        sel = jnp.logical_or(kc > tau, jnp.logical_and(eq, rank <= need))
        madd_scr[c] = jnp.where(sel, 0.0, NEG_MASK)
        return run + jnp.sum(e, axis=0, keepdims=True)

    lax.fori_loop(0, nch, select_chunk, jnp.zeros((1, qb), F32))

    m_scr[...] = jnp.full(m_scr.shape, -jnp.inf, F32)
    den_scr[...] = jnp.zeros(den_scr.shape, F32)
    acc_scr[...] = jnp.zeros(acc_scr.shape, F32)
    qa_all = qa_ref[...].reshape(ATT_HEADS * qb, KV_LORA)
    halves = qb // LANES

    def att_chunk(c, with_bias):
        off = pl.multiple_of(c * cw, cw)
        ckv_c = ckv_ref[pl.ds(off, cw), :]
        boff = pl.multiple_of(jnp.where(c == i, qb, 0), qb)
        l_scr[...] = _bdot_nt(ckv_c, qa_all)
        for g in range(ATT_HEADS * halves):
            cols = slice(g * LANES, (g + 1) * LANES)
            qcols = slice((g % halves) * LANES, (g % halves + 1) * LANES)
            l = l_scr[:, cols] + madd_scr[c, :, qcols]
            if with_bias:
                l = l + near_ref[g // halves, pl.ds(boff, qb), qcols]
            m_old = m_scr[:, cols]
            m_new = jnp.maximum(m_old, jnp.max(l, axis=0, keepdims=True))
            p = jnp.exp(l - m_new)
            scale = jnp.exp(m_old - m_new)
            den_scr[:, cols] = den_scr[:, cols] * scale + jnp.sum(p, axis=0, keepdims=True)
            m_scr[:, cols] = m_new
            sc_scr[:, cols] = scale
            p_scr[:, cols] = p.astype(BF16)
        pv = jnp.dot(ckvt_ref[c], p_scr[...], preferred_element_type=F32)
        acc_scr[...] = acc_scr[...] * sc_scr[...] + pv

    def far_body(c, carry):
        att_chunk(c, False)
        return carry

    def near_body(c, carry):
        att_chunk(c, True)
        return carry

    c_near = jnp.maximum(i - 1, 0)
    lax.fori_loop(0, c_near, far_body, 0)
    lax.fori_loop(c_near, nch, near_body, 0)

    o_rows = []
    for h in range(ATT_HEADS):
        o_lat_t = acc_scr[:, h * qb:(h + 1) * qb] / den_scr[:, h * qb:(h + 1) * qb]
        o_rows.append(_bdot(wuvt_ref[h], o_lat_t))
    h_out = _bdot(jnp.concatenate(o_rows, axis=0).T, wo_ref[...])
    o_ref[...] = _layer_norm(alpha * x_ref[...] + h_out, lng_ref[...], lnb_ref[...])


def _dsa_attn(x2d, batch, seq, qa, qi, ki, wit, ckv, ckvt, near, w_uv, w_o, ln_g, ln_b, alpha):
    t, d = x2d.shape
    qb = DSA_ROWS
    assert KV_LORA == LANES
    nb = seq // qb
    topk = min(TOPK_MAX, seq // 4)
    tril = (jnp.arange(qb)[:, None] >= jnp.arange(qb)[None, :]).astype(BF16)
    wuv_t = w_uv.transpose(0, 2, 1).astype(BF16)
    wo = w_o.astype(BF16)
    tok = lambda b, i: (b * nb + i, 0)
    in_specs = [
        pl.BlockSpec((IDX_HEADS, qb, IDX_DIM), lambda b, i: (0, b * nb + i, 0)),
        pl.BlockSpec((IDX_HEADS, qb), lambda b, i: (0, b * nb + i)),
        pl.BlockSpec((seq, IDX_DIM), lambda b, i: (b, 0)),
        pl.BlockSpec((seq, KV_LORA), lambda b, i: (b, 0)),
        pl.BlockSpec((nb, KV_LORA, qb), lambda b, i: (b, 0, 0)),
        pl.BlockSpec((ATT_HEADS, qb, KV_LORA), lambda b, i: (0, b * nb + i, 0)),
        pl.BlockSpec(near.shape, lambda b, i: (0, 0, 0), pipeline_mode=pl.Buffered(1)),
        _const_spec(tril.shape),
        _const_spec(wuv_t.shape),
        _const_spec(wo.shape),
        pl.BlockSpec((qb, d), tok),
        _const_spec((1, d)),
        _const_spec((1, d)),
    ]
    scratch = [pltpu.VMEM((nb, qb, qb), jnp.int32),
               pltpu.VMEM((nb, qb, qb), jnp.int16),
               pltpu.VMEM((nb, qb, qb), jnp.int16),
               pltpu.VMEM((nb, qb, qb), F32),
               pltpu.VMEM((1, ATT_HEADS * qb), F32),
               pltpu.VMEM((1, ATT_HEADS * qb), F32),
               pltpu.VMEM((KV_LORA, ATT_HEADS * qb), F32),
               pltpu.VMEM((1, ATT_HEADS * qb), F32),
               pltpu.VMEM((qb, ATT_HEADS * qb), F32),
               pltpu.VMEM((qb, ATT_HEADS * qb), BF16)]
    return pl.pallas_call(
        functools.partial(_dsa_attn_kernel, topk=topk, qb=qb, alpha=alpha),
        grid=(batch, nb),
        in_specs=in_specs,
        out_specs=pl.BlockSpec((qb, d), tok),
        out_shape=jax.ShapeDtypeStruct((t, d), F32),
        scratch_shapes=scratch,
        compiler_params=pltpu.CompilerParams(dimension_semantics=("parallel", "parallel"),
                                             vmem_limit_bytes=VMEM_LIMIT),
        name="dsa_attn",
    )(qi, wit, ki, ckv, ckvt, qa, near, tril, wuv_t, wo, x2d, ln_g.reshape(1, d), ln_b.reshape(1, d))


def kernel(x, ln_g, ln_b, rwkv_mix, rwkv_w_rkv, rwkv_w0, rwkv_w1, rwkv_w2, rwkv_a0, rwkv_a1, rwkv_a2, rwkv_v0, rwkv_v1, rwkv_v2, rwkv_g1, rwkv_g2, rwkv_k_k, rwkv_k_a, rwkv_r_k, rwkv_lnx_g, rwkv_lnx_b, rwkv_w_o, dsa_w_in, dsa_q_norm_g, dsa_kv_norm_g, dsa_w_uq, dsa_w_uk, dsa_w_uv, dsa_w_qidx, dsa_kidx_g, dsa_kidx_b, dsa_w_o, rel_bias, ffn_w_up, ffn_conv_w, ffn_conv_b, ffn_w_down):
    batch, seq, d = x.shape
    depth = ln_g.shape[0]
    heads = d // N_HEAD
    alpha = (2 * depth) ** 0.25
    gb = LANES // heads
    groups = batch // gb
    t = gb * seq
    xs = [x[g * gb:(g + 1) * gb].reshape(t, d) for g in range(groups)]

    p = (jnp.arange(d)[:, None] // N_HEAD == jnp.arange(LANES)[None, :]).astype(BF16)
    pt = p.T
    near = _bias_tiles(rel_bias, DSA_ROWS)

    to_scan = functools.partial(_to_scan, gb=gb, seq=seq)

    v_first = [None] * groups
    for i in range(depth):
        j = i // 2
        for g in range(groups):
            x2d = xs[g]
            if i % 2 == 0:
                vres = None if j == 0 else (v_first[g], rwkv_v0[j - 1], rwkv_v1[j - 1], rwkv_v2[j - 1])
                r, w, k, v, a_s, b_s, gate = _rwkv_proj(
                    x2d, seq, rwkv_mix[j], rwkv_w_rkv[j], rwkv_w0[j], rwkv_w1[j], rwkv_w2[j], rwkv_a0[j],
                    rwkv_a1[j], rwkv_a2[j], rwkv_g1[j], rwkv_g2[j], rwkv_k_k[j], rwkv_k_a[j], p, pt, vres)
                if j == 0:
                    v_first[g] = v
                y = _wkv(to_scan(r), to_scan(w), to_scan(k), to_scan(v), to_scan(a_s), to_scan(b_s))
                x2d = _rwkv_out(_from_scan(y, gb, d), r, k, v, gate, x2d, rwkv_r_k[j], rwkv_lnx_g[j], rwkv_lnx_b[j],
                                rwkv_w_o[j], ln_g[i, 0], ln_b[i, 0], p, pt, alpha)
            else:
                qa, qi, ki, wit, ckv, ckvt = _dsa_proj(x2d, dsa_w_in[j], dsa_q_norm_g[j], dsa_kv_norm_g[j],
                                                       dsa_w_uq[j], dsa_w_uk[j], dsa_w_qidx[j], dsa_kidx_g[j],
                                                       dsa_kidx_b[j])
                x2d = _dsa_attn(x2d, gb, seq, qa, qi, ki, wit, ckv, ckvt, near, dsa_w_uv[j], dsa_w_o[j],
                                ln_g[i, 0], ln_b[i, 0], alpha)
            xs[g] = _ffn(x2d, seq, ffn_w_up[i], ffn_conv_w[i], ffn_conv_b[i], ffn_w_down[i],
                         ln_g[i, 1], ln_b[i, 1], alpha)
    return jnp.concatenate([z.reshape(gb, seq, d) for z in xs], axis=0)
```

```python
import functools
import math

import jax
import jax.numpy as jnp
from jax import lax
from jax.experimental import pallas as pl
from jax.experimental.pallas import tpu as pltpu

F32 = jnp.float32
BF16 = jnp.bfloat16

N_HEAD = 64
ATT_HEADS = 16
QK_HEAD = 64
V_HEAD = 64
Q_LORA = 256
KV_LORA = 128
IDX_HEADS = 8
IDX_DIM = 64
TOPK_MAX = 256
DSA_ROWS = 256
REL_BUCKETS = 32
REL_MAX_DIST = 128
CONV_W = 3
GN_EPS = 64e-5
LN_EPS = 1e-5

LANES = 128
SUBLANES = 8
VMEM_LIMIT = 56 * 1024 * 1024
PAIR_PITCH = N_HEAD + 8
TIME_PITCH = LANES + 8
WKV_UNROLL = 8

NEG_MASK = -1e30
INT_MIN = -2 ** 31


def _bdot(a, b):
    return jnp.dot(a.astype(BF16), b.astype(BF16), preferred_element_type=F32)


def _bdot_nt(a, b):
    return lax.dot_general(a.astype(BF16), b.astype(BF16), (((1,), (1,)), ((), ())),
                           preferred_element_type=F32)


def _layer_norm(x, g, b):
    mu = jnp.mean(x, axis=-1, keepdims=True)
    xc = x - mu
    var = jnp.mean(xc * xc, axis=-1, keepdims=True)
    return xc * lax.rsqrt(var + LN_EPS) * g + b


def _const_spec(shape):
    nd = len(shape)
    return pl.BlockSpec(shape, lambda *_: (0,) * nd)


def _prev_rows_spec(ts, d):
    return pl.BlockSpec((SUBLANES, d), lambda i: (jnp.maximum(i * (ts // SUBLANES) - 1, 0), 0))


def _rwkv_proj_kernel(*refs, ts, seq, has_vres):
    (xp_ref, x_ref, mix_ref, wr_ref, wk_ref, wv_ref, w0_ref, w1_ref, w2_ref, a0_ref, a1_ref,
     a2_ref, g1_ref, g2_ref, kk_ref, ka_ref) = refs[:16]
    if has_vres:
        vf_ref, v0_ref, v1_ref, v2_ref = refs[16:20]
        outs = refs[20:]
    else:
        outs = refs[16:]
    r_o, w_o, k_o, v_o, kk_o, a_o, g_o = outs

    first = (pl.program_id(0) * ts) % seq == 0
    x = x_ref[...]
    p_last = jnp.where(first, 0.0, xp_ref[SUBLANES - 1:SUBLANES, :])
    rows = lax.broadcasted_iota(jnp.int32, (ts, 1), 0)
    x_shift = jnp.where(rows == 0, p_last, pltpu.roll(x, 1, 0))
    xx = x_shift - x
    mix = mix_ref[...]

    def xm(j):
        return x + xx * mix[j:j + 1]

    r = _bdot(xm(0), wr_ref[...])
    k = _bdot(xm(1), wk_ref[...])
    xv = xm(2)
    v = _bdot(xv, wv_ref[...])
    w_log = -jax.nn.softplus(-(w0_ref[...] + _bdot(jnp.tanh(_bdot(xm(3), w1_ref[...])), w2_ref[...]))) - 0.5
    decay = jnp.exp(-jnp.exp(w_log))
    if has_vres:
        v = v + (vf_ref[...] - v) * jax.nn.sigmoid(v0_ref[...] + _bdot(_bdot(xv, v1_ref[...]), v2_ref[...]))
    a = jax.nn.sigmoid(a0_ref[...] + _bdot(_bdot(xm(4), a1_ref[...]), a2_ref[...]))
    g = _bdot(jax.nn.sigmoid(_bdot(xm(5), g1_ref[...])), g2_ref[...])
    r_o[...] = r
    w_o[...] = decay
    k_o[...] = k * (1.0 + (a - 1.0) * ka_ref[...])
    v_o[...] = v
    kk_o[...] = k * kk_ref[...]
    a_o[...] = a
    g_o[...] = g


def _rwkv_proj(x2d, seq, mix, w_rkv, w0, w1, w2, a0, a1, a2, g1, g2, k_k, k_a, vres, ts=256):
    t, d = x2d.shape
    row = lambda v: v.reshape(1, d)
    tok = pl.BlockSpec((ts, d), lambda i: (i, 0))
    ins = [x2d, x2d, mix, w_rkv[0].astype(BF16), w_rkv[1].astype(BF16), w_rkv[2].astype(BF16),
           row(w0), w1.astype(BF16), w2.astype(BF16), row(a0), a1.astype(BF16), a2.astype(BF16),
           g1.astype(BF16), g2.astype(BF16), row(k_k), row(k_a)]
    specs = [_prev_rows_spec(ts, d), tok] + [_const_spec(a.shape) for a in ins[2:]]
    if vres is not None:
        v_first, v0, v1, v2 = vres
        extra = [v_first, row(v0), v1.astype(BF16), v2.astype(BF16)]
        ins += extra
        specs += [tok] + [_const_spec(a.shape) for a in extra[1:]]
    out_shape = [jax.ShapeDtypeStruct((t, d), F32)] * 7
    return pl.pallas_call(
        functools.partial(_rwkv_proj_kernel, ts=ts, seq=seq, has_vres=vres is not None),
        grid=(t // ts,),
        in_specs=specs,
        out_specs=[tok] * 7,
        out_shape=out_shape,
        compiler_params=pltpu.CompilerParams(dimension_semantics=("parallel",),
                                             vmem_limit_bytes=VMEM_LIMIT),
        name="rwkv_proj",
    )(*ins)


def _to_scan_kernel(x_ref, o_ref, zt_scr, w_scr, *, gb, d, n):
    heads = d // n
    per_tile = LANES // n
    for b in range(gb):
        for j in range(d // LANES):
            xt = x_ref[b, :, j * LANES:(j + 1) * LANES].T
            for hh in range(per_tile):
                h = j * per_tile + hh
                zt_scr[pl.ds((b * heads + h) * PAIR_PITCH, n), :] = xt[hh * n:(hh + 1) * n]
    tt = o_ref.shape[0]
    for c in range(n):
        w_scr[pl.ds(c * TIME_PITCH, tt), :] = zt_scr[pl.ds(c, gb * heads, stride=PAIR_PITCH), :].T
    for t in range(tt):
        o_ref[t] = w_scr[pl.ds(t, n, stride=TIME_PITCH), :]


def _to_scan(x2d, gb, seq):
    t, d = x2d.shape
    tt = LANES
    n = N_HEAD
    pairs = gb * d // n
    return pl.pallas_call(
        functools.partial(_to_scan_kernel, gb=gb, d=d, n=n),
        grid=(seq // tt,),
        in_specs=[pl.BlockSpec((gb, tt, d), lambda i: (0, i, 0))],
        out_specs=pl.BlockSpec((tt, n, pairs), lambda i: (i, 0, 0)),
        out_shape=jax.ShapeDtypeStruct((seq, n, pairs), F32),
        scratch_shapes=[pltpu.VMEM((pairs * PAIR_PITCH, tt), F32), pltpu.VMEM((n * TIME_PITCH, pairs), F32)],
        compiler_params=pltpu.CompilerParams(dimension_semantics=("parallel",), vmem_limit_bytes=VMEM_LIMIT),
        name="to_scan_layout",
    )(x2d.reshape(gb, seq, d))


def _from_scan_kernel(y_ref, o_ref, zt_scr, w_scr, *, gb, d, n):
    heads = d // n
    tt = y_ref.shape[0]
    for t in range(tt):
        w_scr[pl.ds(t, n, stride=TIME_PITCH), :] = y_ref[t]
    for c in range(n):
        zt_scr[pl.ds(c, gb * heads, stride=PAIR_PITCH), :] = w_scr[pl.ds(c * TIME_PITCH, tt), :].T
    per_tile = LANES // n
    for b in range(gb):
        for j in range(d // LANES):
            xt = jnp.concatenate([zt_scr[pl.ds((b * heads + j * per_tile + hh) * PAIR_PITCH, n), :]
                                  for hh in range(per_tile)], axis=0)
            o_ref[b, :, j * LANES:(j + 1) * LANES] = xt.T


def _from_scan(y, gb, d):
    seq, n, pairs = y.shape
    tt = LANES
    return pl.pallas_call(
        functools.partial(_from_scan_kernel, gb=gb, d=d, n=n),
        grid=(seq // tt,),
        in_specs=[pl.BlockSpec((tt, n, pairs), lambda i: (i, 0, 0))],
        out_specs=pl.BlockSpec((gb, tt, d), lambda i: (0, i, 0)),
        out_shape=jax.ShapeDtypeStruct((gb, seq, d), F32),
        scratch_shapes=[pltpu.VMEM((pairs * PAIR_PITCH, tt), F32), pltpu.VMEM((n * TIME_PITCH, pairs), F32)],
        compiler_params=pltpu.CompilerParams(dimension_semantics=("parallel",), vmem_limit_bytes=VMEM_LIMIT),
        name="from_scan_layout",
    )(y).reshape(gb * seq, d)


def _rows_of_sums(parts):
    row = lax.broadcasted_iota(jnp.int32, parts[0].shape, 0)
    dist = SUBLANES // 2
    while dist >= 1:
        lower = (row & dist) == 0
        half = len(parts) // 2
        parts = [jnp.where(lower, parts[j] + pltpu.roll(parts[j], SUBLANES - dist, 0),
                           parts[j + half] + pltpu.roll(parts[j + half], dist, 0)) for j in range(half)]
        dist //= 2
    return parts[0]


def _wkv_kernel(r_ref, w_ref, k_ref, v_ref, kk_ref, al_ref, rk_ref, lxg_ref, lxb_ref, y_ref, s_scr, *, tc, n):
    @pl.when(pl.program_id(1) == 0)
    def _():
        s_scr[...] = jnp.zeros_like(s_scr)

    groups = n // SUBLANES
    inv_n = 1.0 / n

    def step(t, carry):
        w = w_ref[t]
        k = k_ref[t]
        r = r_ref[t]
        kk = kk_ref[t]
        kk = kk / jnp.maximum(jnp.sqrt(jnp.sum(kk * kk, axis=0, keepdims=True)), 1e-12)
        a = -kk
        b = kk * al_ref[t]
        wr = w * r
        br = jnp.sum(b * r, axis=0, keepdims=True)
        kr = jnp.sum(k * r, axis=0, keepdims=True)

        def vblock(vb, c):
            base = pl.multiple_of(vb * SUBLANES, SUBLANES)
            pa, py = [], []
            for j in range(SUBLANES):
                s = s_scr[base + j]
                pa.append(jnp.sum((s * a).reshape(groups, SUBLANES, LANES), axis=0))
                py.append(jnp.sum((s * wr).reshape(groups, SUBLANES, LANES), axis=0))
            sa = _rows_of_sums(pa)
            vv = v_ref[t, pl.ds(base, SUBLANES), :]
            y_ref[t, pl.ds(base, SUBLANES), :] = _rows_of_sums(py) + sa * br + vv * kr
            for j in range(SUBLANES):
                s_scr[base + j] = s_scr[base + j] * w + sa[j:j + 1] * b + vv[j:j + 1] * k
            return c

        lax.fori_loop(0, groups, vblock, 0, unroll=WKV_UNROLL)

        y = y_ref[t]
        yc = y - jnp.sum(y, axis=0, keepdims=True) * inv_n
        var = jnp.sum(yc * yc, axis=0, keepdims=True) * inv_n
        bonus = jnp.sum(r * k * rk_ref[...], axis=0, keepdims=True) * v_ref[t]
        y_ref[t] = yc * lax.rsqrt(var + GN_EPS) * lxg_ref[...] + lxb_ref[...] + bonus
        return carry

    lax.fori_loop(0, tc, step, 0)


def _wkv(r, w, k, v, kk, alr, rk, lxg, lxb, tc=16):
    s, n, pairs = r.shape
    spec = pl.BlockSpec((tc, n, LANES), lambda p, c: (c, 0, p))
    pspec = pl.BlockSpec((n, LANES), lambda p, c: (0, p))
    return pl.pallas_call(
        functools.partial(_wkv_kernel, tc=tc, n=n),
        grid=(pairs // LANES, s // tc),
        in_specs=[spec] * 6 + [pspec] * 3,
        out_specs=spec,
        out_shape=jax.ShapeDtypeStruct((s, n, pairs), F32),
        scratch_shapes=[pltpu.VMEM((n, n, LANES), F32)],
        compiler_params=pltpu.CompilerParams(dimension_semantics=("parallel", "arbitrary"),
                                             vmem_limit_bytes=VMEM_LIMIT),
        name="wkv7_scan",
    )(r, w, k, v, kk, alr, rk, lxg, lxb)


def _rwkv_out_kernel(z_ref, g_ref, x_ref, wo_ref, lng_ref, lnb_ref, o_ref, *, alpha):
    h = _bdot(z_ref[...] * g_ref[...], wo_ref[...])
    o_ref[...] = _layer_norm(alpha * x_ref[...] + h, lng_ref[...], lnb_ref[...])


def _rwkv_out(z, g, x2d, w_o, ln_g, ln_b, alpha, ts=512):
    t, d = x2d.shape
    row = lambda a: a.reshape(1, d)
    tok = pl.BlockSpec((ts, d), lambda i: (i, 0))
    consts = [w_o.astype(BF16), row(ln_g), row(ln_b)]
    return pl.pallas_call(
        functools.partial(_rwkv_out_kernel, alpha=alpha),
        grid=(t // ts,),
        in_specs=[tok] * 3 + [_const_spec(a.shape) for a in consts],
        out_specs=tok,
        out_shape=jax.ShapeDtypeStruct((t, d), F32),
        compiler_params=pltpu.CompilerParams(dimension_semantics=("parallel",),
                                             vmem_limit_bytes=VMEM_LIMIT),
        name="rwkv_out",
    )(z, g, x2d, *consts)


def _ffn_kernel(xp_ref, x_ref, wup_ref, cw_ref, cb_ref, wdn_ref, lng_ref, lnb_ref, o_ref, u_scr,
                *, ts, seq, d_ff, fc, alpha):
    first = (pl.program_id(0) * ts) % seq == 0
    x = x_ref[...]
    xp = jnp.where(first, 0.0, xp_ref[...])
    xe = jnp.concatenate([xp, x], axis=0).astype(BF16)

    def conv_cols(off):
        u_scr[...] = jnp.dot(xe, wup_ref[:, off:off + fc], preferred_element_type=F32)
        cw = cw_ref[:, off:off + fc]
        return (u_scr[SUBLANES - 2:SUBLANES - 2 + ts, :] * cw[0:1]
                + u_scr[SUBLANES - 1:SUBLANES - 1 + ts, :] * cw[1:2]
                + u_scr[SUBLANES:SUBLANES + ts, :] * cw[2:3]
                + cb_ref[:, off:off + fc])

    acc = jnp.zeros(x.shape, F32)
    for c in range(d_ff // fc):
        gate = conv_cols(c * fc)
        val = conv_cols(d_ff + c * fc)
        act = jax.nn.silu(gate) * val
        acc = acc + jnp.dot(act.astype(BF16), wdn_ref[c * fc:(c + 1) * fc, :], preferred_element_type=F32)
    o_ref[...] = _layer_norm(alpha * x + acc, lng_ref[...], lnb_ref[...])


def _ffn(x2d, seq, w_up, conv_w, conv_b, w_down, ln_g, ln_b, alpha, ts=512, fc=1408):
    t, d = x2d.shape
    d_ff = w_down.shape[0]
    tok = pl.BlockSpec((ts, d), lambda i: (i, 0))
    single = pl.Buffered(1)
    wup_spec = pl.BlockSpec((d, 2 * d_ff), lambda i: (0, 0), pipeline_mode=single)
    wdn_spec = pl.BlockSpec((d_ff, d), lambda i: (0, 0), pipeline_mode=single)
    consts = [conv_w, conv_b.reshape(1, -1)]
    rows = [ln_g.reshape(1, d), ln_b.reshape(1, d)]
    return pl.pallas_call(
        functools.partial(_ffn_kernel, ts=ts, seq=seq, d_ff=d_ff, fc=fc, alpha=alpha),
        grid=(t // ts,),
        in_specs=[_prev_rows_spec(ts, d), tok, wup_spec] + [_const_spec(a.shape) for a in consts]
                 + [wdn_spec] + [_const_spec(a.shape) for a in rows],
        out_specs=tok,
        out_shape=jax.ShapeDtypeStruct((t, d), F32),
        scratch_shapes=[pltpu.VMEM((ts + SUBLANES, fc), F32)],
        compiler_params=pltpu.CompilerParams(dimension_semantics=("parallel",),
                                             vmem_limit_bytes=VMEM_LIMIT),
        name="conv_ffn",
    )(x2d, x2d, w_up.astype(BF16), *consts, w_down.astype(BF16), *rows)


def _dsa_proj_kernel(x_ref, wcq_ref, wckv_ref, wki_ref, wwit_ref, qg_ref, kvg_ref, wuq_ref, wukp_ref,
                     wqi_ref, kig_ref, kib_ref, qa_o, qi_o, ki_o, wit_o, ckv_o, ckvt_o, *, cw):
    x = x_ref[...].astype(BF16)
    c_q = jnp.dot(x, wcq_ref[...], preferred_element_type=F32)
    c_kv = jnp.dot(x, wckv_ref[...], preferred_element_type=F32)
    k_idx = jnp.dot(x, wki_ref[...], preferred_element_type=F32)
    w_idx_t = _bdot_nt(wwit_ref[...], x)

    def rms(z, g):
        return z * lax.rsqrt(jnp.mean(z * z, axis=-1, keepdims=True) + 1e-6) * g

    c_q = rms(c_q, qg_ref[...]).astype(BF16)
    c_kv = rms(c_kv, kvg_ref[...])
    ckv_o[...] = c_kv.astype(BF16)
    for jc in range(c_kv.shape[0] // cw):
        ckvt_o[jc] = c_kv[jc * cw:(jc + 1) * cw].T.astype(BF16)
    q = jnp.dot(c_q, wuq_ref[...], preferred_element_type=F32)
    qk_scale = QK_HEAD ** -0.5
    for hp in range(ATT_HEADS // 2):
        qa = _bdot(q[:, hp * LANES:(hp + 1) * LANES], wukp_ref[hp]) * qk_scale
        qa_o[2 * hp] = qa[:, :KV_LORA].astype(BF16)
        qa_o[2 * hp + 1] = qa[:, KV_LORA:].astype(BF16)
    for h in range(IDX_HEADS):
        qi_o[h] = jnp.dot(c_q, wqi_ref[h], preferred_element_type=F32).astype(BF16)
    ki_o[...] = _layer_norm(k_idx, kig_ref[...], kib_ref[...]).astype(BF16)
    wit_o[...] = w_idx_t * (IDX_HEADS ** -0.5 * IDX_DIM ** -0.5)


def _dsa_proj(x2d, w_in, q_norm_g, kv_norm_g, w_uq, w_uk, w_qidx, kidx_g, kidx_b, ts=512):
    t, d = x2d.shape
    cw = DSA_ROWS
    o1, o2, o3 = Q_LORA, Q_LORA + KV_LORA, Q_LORA + KV_LORA + IDX_DIM
    w_in = w_in.astype(BF16)
    z = jnp.zeros((ATT_HEADS // 2, QK_HEAD, KV_LORA), F32)
    wuk_pair = jnp.concatenate([jnp.concatenate([w_uk[0::2], z], axis=2),
                                jnp.concatenate([z, w_uk[1::2]], axis=2)], axis=1).astype(BF16)
    wqi = w_qidx.reshape(Q_LORA, IDX_HEADS, IDX_DIM).transpose(1, 0, 2).astype(BF16)
    consts = [w_in[:, :o1], w_in[:, o1:o2], w_in[:, o2:o3], w_in[:, o3:].T, q_norm_g.reshape(1, -1),
              kv_norm_g.reshape(1, -1), w_uq.astype(BF16), wuk_pair, wqi, kidx_g.reshape(1, -1),
              kidx_b.reshape(1, -1)]
    out_shape = [jax.ShapeDtypeStruct((ATT_HEADS, t, KV_LORA), BF16),
                 jax.ShapeDtypeStruct((IDX_HEADS, t, IDX_DIM), BF16),
                 jax.ShapeDtypeStruct((t, IDX_DIM), BF16),
                 jax.ShapeDtypeStruct((IDX_HEADS, t), F32),
                 jax.ShapeDtypeStruct((t, KV_LORA), BF16),
                 jax.ShapeDtypeStruct((t // cw, KV_LORA, cw), BF16)]
    out_specs = [pl.BlockSpec((ATT_HEADS, ts, KV_LORA), lambda i: (0, i, 0)),
                 pl.BlockSpec((IDX_HEADS, ts, IDX_DIM), lambda i: (0, i, 0)),
                 pl.BlockSpec((ts, IDX_DIM), lambda i: (i, 0)),
                 pl.BlockSpec((IDX_HEADS, ts), lambda i: (0, i)),
                 pl.BlockSpec((ts, KV_LORA), lambda i: (i, 0)),
                 pl.BlockSpec((ts // cw, KV_LORA, cw), lambda i: (i, 0, 0))]
    return pl.pallas_call(
        functools.partial(_dsa_proj_kernel, cw=cw),
        grid=(t // ts,),
        in_specs=[pl.BlockSpec((ts, d), lambda i: (i, 0))] + [_const_spec(a.shape) for a in consts],
        out_specs=out_specs,
        out_shape=out_shape,
        compiler_params=pltpu.CompilerParams(dimension_semantics=("parallel",),
                                             vmem_limit_bytes=VMEM_LIMIT),
        name="dsa_proj",
    )(x2d, *consts)


def _bias_tiles_kernel(rb_ref, o_ref, *, qb):
    c = lax.broadcasted_iota(jnp.int32, (2 * qb, qb), 0)
    tl = lax.broadcasted_iota(jnp.int32, (2 * qb, qb), 1)
    n = jnp.maximum(qb + tl - c, 0)
    max_exact = REL_BUCKETS // 2
    nf = jnp.maximum(n, 1).astype(F32)
    large = max_exact + (jnp.log(nf / max_exact) / math.log(REL_MAX_DIST / max_exact)
                         * (REL_BUCKETS - max_exact)).astype(jnp.int32)
    large = jnp.minimum(large, REL_BUCKETS - 1)
    bucket = jnp.where(n < max_exact, n, large)
    for h in range(ATT_HEADS):
        acc = jnp.zeros((2 * qb, qb), F32)
        for bkt in range(REL_BUCKETS):
            acc = jnp.where(bucket == bkt, rb_ref[bkt, h], acc)
        o_ref[h] = acc - rb_ref[REL_BUCKETS - 1, h]


def _bias_tiles(rel_bias, qb):
    return pl.pallas_call(
        functools.partial(_bias_tiles_kernel, qb=qb),
        in_specs=[pl.BlockSpec(memory_space=pltpu.SMEM)],
        out_specs=pl.BlockSpec(memory_space=pltpu.VMEM),
        out_shape=jax.ShapeDtypeStruct((ATT_HEADS, 2 * qb, qb), F32),
        name="rel_bias_tiles",
    )(rel_bias)


def _dsa_attn_kernel(qi_ref, wit_ref, ki_ref, ckv_ref, ckvt_ref, qa_ref, near_ref, tril_ref, wuvt_ref,
                     wo_ref, x_ref, lng_ref, lnb_ref, o_ref, key_scr, khi_scr, klo_scr, madd_scr, m_scr, den_scr, acc_scr,
                     sc_scr, l_scr, p_scr,
                     *, topk, qb, alpha):
    cw = qb
    i = pl.program_id(1)
    nch = i + 1
    t_pos = i * qb + lax.broadcasted_iota(jnp.int32, (1, qb), 1)
    s_loc = lax.broadcasted_iota(jnp.int32, (cw, 1), 0)
    int_min = jnp.int32(INT_MIN)

    qi_all = qi_ref[...].reshape(IDX_HEADS * qb, IDX_DIM)
    wit = wit_ref[...]

    def score_chunk(c, carry):
        off = pl.multiple_of(c * cw, cw)
        s_all = _bdot_nt(ki_ref[pl.ds(off, cw), :], qi_all)
        score = jnp.zeros((cw, qb), F32)
        for h in range(IDX_HEADS):
            score = score + jnp.maximum(s_all[:, h * qb:(h + 1) * qb], 0.0) * wit[h:h + 1]
        bits = lax.bitcast_convert_type(score, jnp.int32)
        key = jnp.where(bits < 0, bits ^ jnp.int32(0x7FFFFFFF), bits)
        key = jnp.where(off + s_loc <= t_pos, key, int_min)
        key_scr[c] = key
        khi_scr[c] = lax.shift_right_arithmetic(key, 16).astype(jnp.int16)
        klo_scr[c] = ((key & 0xFFFF) - 32768).astype(jnp.int16)
        return carry

    lax.fori_loop(0, nch, score_chunk, 0)

    def count(pred):
        def body(c, acc):
            kc = key_scr[c]
            for r in range(cw // SUBLANES):
                acc = acc + jnp.where(pred(kc[r * SUBLANES:(r + 1) * SUBLANES]), 1.0, 0.0)
            return acc

        acc = lax.fori_loop(0, nch, body, jnp.zeros((SUBLANES, qb), F32))
        return jnp.sum(acc, axis=0, keepdims=True)

    rows16 = 2 * SUBLANES
    one16, zero16, min16 = jnp.int16(1), jnp.int16(0), jnp.int16(-32768)

    def count16(scr, pred):
        def body(c, acc):
            kc = scr[c]
            for r in range(cw // rows16):
                acc = acc + jnp.where(pred(kc[r * rows16:(r + 1) * rows16]), one16, zero16)
            return acc

        acc = lax.fori_loop(0, nch, body, jnp.zeros((rows16, qb), jnp.int16))
        return jnp.sum(acc.astype(jnp.int32), axis=0, keepdims=True)

    def search16(scr, want):
        tau = jnp.where(count16(scr, lambda x: x >= zero16) >= want, jnp.int32(0), jnp.int32(-32768))

        def bit_step(bi, tau):
            cand = tau | lax.shift_left(jnp.int32(1), jnp.int32(14) - bi)
            c16 = cand.astype(jnp.int16)
            return jnp.where(count16(scr, lambda x: x >= c16) >= want, cand, tau)

        return lax.fori_loop(0, 15, bit_step, tau)

    tau_hi = search16(khi_scr, jnp.int32(topk))
    th16 = tau_hi.astype(jnp.int16)
    want_lo = topk - count16(khi_scr, lambda x: x > th16)

    def mask_chunk(c, carry):
        klo_scr[c] = jnp.where(khi_scr[c] == th16, klo_scr[c], min16)
        return carry

    lax.fori_loop(0, nch, mask_chunk, 0)
    tau_lo = search16(klo_scr, want_lo)
    tau = lax.shift_left(tau_hi, 16) | ((tau_lo + 32768) & 0xFFFF)
    kf = float(topk)

    need = kf - count(lambda kc: kc > tau)
    has_kth = tau > int_min
    tril = tril_ref[...]

    def select_chunk(c, run):
        kc = key_scr[c]
        eq = jnp.logical_and(kc == tau, has_kth)
        e = jnp.where(eq, 1.0, 0.0)
        rank = jnp.dot(tril, e.astype(BF16), preferred_element_type=F32) + run
        sel = jnp.logical_or(kc > tau, jnp.logical_and(eq, rank <= need))
        madd_scr[c] = jnp.where(sel, 0.0, NEG_MASK)
        return run + jnp.sum(e, axis=0, keepdims=True)

    lax.fori_loop(0, nch, select_chunk, jnp.zeros((1, qb), F32))

    m_scr[...] = jnp.full(m_scr.shape, -jnp.inf, F32)
    den_scr[...] = jnp.zeros(den_scr.shape, F32)
    acc_scr[...] = jnp.zeros(acc_scr.shape, F32)
    qa_all = qa_ref[...].reshape(ATT_HEADS * qb, KV_LORA)
    halves = qb // LANES

    def att_chunk(c, with_bias):
        off = pl.multiple_of(c * cw, cw)
        ckv_c = ckv_ref[pl.ds(off, cw), :]
        boff = pl.multiple_of(jnp.where(c == i, qb, 0), qb)
        l_scr[...] = _bdot_nt(ckv_c, qa_all)
        for g in range(ATT_HEADS * halves):
            cols = slice(g * LANES, (g + 1) * LANES)
            qcols = slice((g % halves) * LANES, (g % halves + 1) * LANES)
            l = l_scr[:, cols] + madd_scr[c, :, qcols]
            if with_bias:
                l = l + near_ref[g // halves, pl.ds(boff, qb), qcols]
            m_old = m_scr[:, cols]
            m_new = jnp.maximum(m_old, jnp.max(l, axis=0, keepdims=True))
            p = jnp.exp(l - m_new)
            scale = jnp.exp(m_old - m_new)
            den_scr[:, cols] = den_scr[:, cols] * scale + jnp.sum(p, axis=0, keepdims=True)
            m_scr[:, cols] = m_new
            sc_scr[:, cols] = scale
            p_scr[:, cols] = p.astype(BF16)
        pv = jnp.dot(ckvt_ref[c], p_scr[...], preferred_element_type=F32)
        acc_scr[...] = acc_scr[...] * sc_scr[...] + pv

    def far_body(c, carry):
        att_chunk(c, False)
        return carry

    def near_body(c, carry):
        att_chunk(c, True)
        return carry

    c_near = jnp.maximum(i - 1, 0)
    lax.fori_loop(0, c_near, far_body, 0)
    lax.fori_loop(c_near, nch, near_body, 0)

    o_rows = []
    for h in range(ATT_HEADS):
        o_lat_t = acc_scr[:, h * qb:(h + 1) * qb] / den_scr[:, h * qb:(h + 1) * qb]
        o_rows.append(_bdot(wuvt_ref[h], o_lat_t))
    h_out = _bdot(jnp.concatenate(o_rows, axis=0).T, wo_ref[...])
    o_ref[...] = _layer_norm(alpha * x_ref[...] + h_out, lng_ref[...], lnb_ref[...])


def _dsa_attn(x2d, batch, seq, qa, qi, ki, wit, ckv, ckvt, near, w_uv, w_o, ln_g, ln_b, alpha):
    t, d = x2d.shape
    qb = DSA_ROWS
    assert KV_LORA == LANES
    nb = seq // qb
    topk = min(TOPK_MAX, seq // 4)
    tril = (jnp.arange(qb)[:, None] >= jnp.arange(qb)[None, :]).astype(BF16)
    wuv_t = w_uv.transpose(0, 2, 1).astype(BF16)
    wo = w_o.astype(BF16)
    tok = lambda b, i: (b * nb + i, 0)
    in_specs = [
        pl.BlockSpec((IDX_HEADS, qb, IDX_DIM), lambda b, i: (0, b * nb + i, 0)),
        pl.BlockSpec((IDX_HEADS, qb), lambda b, i: (0, b * nb + i)),
        pl.BlockSpec((seq, IDX_DIM), lambda b, i: (b, 0)),
        pl.BlockSpec((seq, KV_LORA), lambda b, i: (b, 0)),
        pl.BlockSpec((nb, KV_LORA, qb), lambda b, i: (b, 0, 0)),
        pl.BlockSpec((ATT_HEADS, qb, KV_LORA), lambda b, i: (0, b * nb + i, 0)),
        pl.BlockSpec(near.shape, lambda b, i: (0, 0, 0), pipeline_mode=pl.Buffered(1)),
        _const_spec(tril.shape),
        _const_spec(wuv_t.shape),
        _const_spec(wo.shape),
        pl.BlockSpec((qb, d), tok),
        _const_spec((1, d)),
        _const_spec((1, d)),
    ]
    scratch = [pltpu.VMEM((nb, qb, qb), jnp.int32),
               pltpu.VMEM((nb, qb, qb), jnp.int16),
               pltpu.VMEM((nb, qb, qb), jnp.int16),
               pltpu.VMEM((nb, qb, qb), F32),
               pltpu.VMEM((1, ATT_HEADS * qb), F32),
               pltpu.VMEM((1, ATT_HEADS * qb), F32),
               pltpu.VMEM((KV_LORA, ATT_HEADS * qb), F32),
               pltpu.VMEM((1, ATT_HEADS * qb), F32),
               pltpu.VMEM((qb, ATT_HEADS * qb), F32),
               pltpu.VMEM((qb, ATT_HEADS * qb), BF16)]
    return pl.pallas_call(
        functools.partial(_dsa_attn_kernel, topk=topk, qb=qb, alpha=alpha),
        grid=(batch, nb),
        in_specs=in_specs,
        out_specs=pl.BlockSpec((qb, d), tok),
        out_shape=jax.ShapeDtypeStruct((t, d), F32),
        scratch_shapes=scratch,
        compiler_params=pltpu.CompilerParams(dimension_semantics=("parallel", "parallel"),
                                             vmem_limit_bytes=VMEM_LIMIT),
        name="dsa_attn",
    )(qi, wit, ki, ckv, ckvt, qa, near, tril, wuv_t, wo, x2d, ln_g.reshape(1, d), ln_b.reshape(1, d))


def kernel(x, ln_g, ln_b, rwkv_mix, rwkv_w_rkv, rwkv_w0, rwkv_w1, rwkv_w2, rwkv_a0, rwkv_a1, rwkv_a2, rwkv_v0, rwkv_v1, rwkv_v2, rwkv_g1, rwkv_g2, rwkv_k_k, rwkv_k_a, rwkv_r_k, rwkv_lnx_g, rwkv_lnx_b, rwkv_w_o, dsa_w_in, dsa_q_norm_g, dsa_kv_norm_g, dsa_w_uq, dsa_w_uk, dsa_w_uv, dsa_w_qidx, dsa_kidx_g, dsa_kidx_b, dsa_w_o, rel_bias, ffn_w_up, ffn_conv_w, ffn_conv_b, ffn_w_down):
    batch, seq, d = x.shape
    depth = ln_g.shape[0]
    heads = d // N_HEAD
    alpha = (2 * depth) ** 0.25
    gb = LANES // heads
    groups = batch // gb
    t = gb * seq
    xs = [x[g * gb:(g + 1) * gb].reshape(t, d) for g in range(groups)]

    def scan_param(v):
        return jnp.tile(v.reshape(heads, N_HEAD).T, (1, gb))

    near = _bias_tiles(rel_bias, DSA_ROWS)

    to_scan = functools.partial(_to_scan, gb=gb, seq=seq)

    v_first = [None] * groups
    for i in range(depth):
        j = i // 2
        for g in range(groups):
            x2d = xs[g]
            if i % 2 == 0:
                vres = None if j == 0 else (v_first[g], rwkv_v0[j - 1], rwkv_v1[j - 1], rwkv_v2[j - 1])
                r, w, k, v, kk, lr, gate = _rwkv_proj(
                    x2d, seq, rwkv_mix[j], rwkv_w_rkv[j], rwkv_w0[j], rwkv_w1[j], rwkv_w2[j], rwkv_a0[j],
                    rwkv_a1[j], rwkv_a2[j], rwkv_g1[j], rwkv_g2[j], rwkv_k_k[j], rwkv_k_a[j], vres)
                if j == 0:
                    v_first[g] = v
                z = _wkv(to_scan(r), to_scan(w), to_scan(k), to_scan(v), to_scan(kk), to_scan(lr),
                         scan_param(rwkv_r_k[j].reshape(-1)), scan_param(rwkv_lnx_g[j]), scan_param(rwkv_lnx_b[j]))
                x2d = _rwkv_out(_from_scan(z, gb, d), gate, x2d, rwkv_w_o[j], ln_g[i, 0], ln_b[i, 0], alpha)
            else:
                qa, qi, ki, wit, ckv, ckvt = _dsa_proj(x2d, dsa_w_in[j], dsa_q_norm_g[j], dsa_kv_norm_g[j],
                                                       dsa_w_uq[j], dsa_w_uk[j], dsa_w_qidx[j], dsa_kidx_g[j],
                                                       dsa_kidx_b[j])
                x2d = _dsa_attn(x2d, gb, seq, qa, qi, ki, wit, ckv, ckvt, near, dsa_w_uv[j], dsa_w_o[j],
                                ln_g[i, 0], ln_b[i, 0], alpha)
            xs[g] = _ffn(x2d, seq, ffn_w_up[i], ffn_conv_w[i], ffn_conv_b[i], ffn_w_down[i],
                         ln_g[i, 1], ln_b[i, 1], alpha)
    return jnp.concatenate([z.reshape(gb, seq, d) for z in xs], axis=0)
```

```python
import functools
import math

import jax
import jax.numpy as jnp
from jax import lax
from jax.experimental import pallas as pl
from jax.experimental.pallas import tpu as pltpu

F32 = jnp.float32
BF16 = jnp.bfloat16

N_HEAD = 64
ATT_HEADS = 16
QK_HEAD = 64
V_HEAD = 64
Q_LORA = 256
KV_LORA = 128
IDX_HEADS = 8
IDX_DIM = 64
TOPK_MAX = 256
DSA_ROWS = 256
REL_BUCKETS = 32
REL_MAX_DIST = 128
CONV_W = 3
GN_EPS = 64e-5
LN_EPS = 1e-5

LANES = 128
SUBLANES = 8
VMEM_LIMIT = 56 * 1024 * 1024
PAIR_PITCH = N_HEAD + 8
TIME_PITCH = LANES + 8
WKV_UNROLL = 8

NEG_MASK = -1e30
INT_MIN = -2 ** 31


def _bdot(a, b):
    return jnp.dot(a.astype(BF16), b.astype(BF16), preferred_element_type=F32)


def _bdot_nt(a, b):
    return lax.dot_general(a.astype(BF16), b.astype(BF16), (((1,), (1,)), ((), ())),
                           preferred_element_type=F32)


def _layer_norm(x, g, b):
    mu = jnp.mean(x, axis=-1, keepdims=True)
    xc = x - mu
    var = jnp.mean(xc * xc, axis=-1, keepdims=True)
    return xc * lax.rsqrt(var + LN_EPS) * g + b


def _const_spec(shape):
    nd = len(shape)
    return pl.BlockSpec(shape, lambda *_: (0,) * nd)


def _prev_rows_spec(ts, d):
    return pl.BlockSpec((SUBLANES, d), lambda i: (jnp.maximum(i * (ts // SUBLANES) - 1, 0), 0))


def _rwkv_proj_kernel(*refs, ts, seq, has_vres):
    (xp_ref, x_ref, mix_ref, wr_ref, wk_ref, wv_ref, w0_ref, w1_ref, w2_ref, a0_ref, a1_ref,
     a2_ref, g1_ref, g2_ref, kk_ref, ka_ref) = refs[:16]
    if has_vres:
        vf_ref, v0_ref, v1_ref, v2_ref = refs[16:20]
        outs = refs[20:]
    else:
        outs = refs[16:]
    r_o, w_o, k_o, v_o, kk_o, a_o, g_o = outs

    first = (pl.program_id(0) * ts) % seq == 0
    x = x_ref[...]
    p_last = jnp.where(first, 0.0, xp_ref[SUBLANES - 1:SUBLANES, :])
    rows = lax.broadcasted_iota(jnp.int32, (ts, 1), 0)
    x_shift = jnp.where(rows == 0, p_last, pltpu.roll(x, 1, 0))
    xx = x_shift - x
    mix = mix_ref[...]

    def xm(j):
        return x + xx * mix[j:j + 1]

    r = _bdot(xm(0), wr_ref[...])
    k = _bdot(xm(1), wk_ref[...])
    xv = xm(2)
    v = _bdot(xv, wv_ref[...])
    w_log = -jax.nn.softplus(-(w0_ref[...] + _bdot(jnp.tanh(_bdot(xm(3), w1_ref[...])), w2_ref[...]))) - 0.5
    decay = jnp.exp(-jnp.exp(w_log))
    if has_vres:
        v = v + (vf_ref[...] - v) * jax.nn.sigmoid(v0_ref[...] + _bdot(_bdot(xv, v1_ref[...]), v2_ref[...]))
    a = jax.nn.sigmoid(a0_ref[...] + _bdot(_bdot(xm(4), a1_ref[...]), a2_ref[...]))
    g = _bdot(jax.nn.sigmoid(_bdot(xm(5), g1_ref[...])), g2_ref[...])
    r_o[...] = r
    w_o[...] = decay
    k_o[...] = k * (1.0 + (a - 1.0) * ka_ref[...])
    v_o[...] = v
    kk_o[...] = k * kk_ref[...]
    a_o[...] = a
    g_o[...] = g


def _rwkv_proj(x2d, seq, mix, w_rkv, w0, w1, w2, a0, a1, a2, g1, g2, k_k, k_a, vres, ts=256):
    t, d = x2d.shape
    row = lambda v: v.reshape(1, d)
    tok = pl.BlockSpec((ts, d), lambda i: (i, 0))
    ins = [x2d, x2d, mix, w_rkv[0].astype(BF16), w_rkv[1].astype(BF16), w_rkv[2].astype(BF16),
           row(w0), w1.astype(BF16), w2.astype(BF16), row(a0), a1.astype(BF16), a2.astype(BF16),
           g1.astype(BF16), g2.astype(BF16), row(k_k), row(k_a)]
    specs = [_prev_rows_spec(ts, d), tok] + [_const_spec(a.shape) for a in ins[2:]]
    if vres is not None:
        v_first, v0, v1, v2 = vres
        extra = [v_first, row(v0), v1.astype(BF16), v2.astype(BF16)]
        ins += extra
        specs += [tok] + [_const_spec(a.shape) for a in extra[1:]]
    out_shape = [jax.ShapeDtypeStruct((t, d), F32)] * 7
    return pl.pallas_call(
        functools.partial(_rwkv_proj_kernel, ts=ts, seq=seq, has_vres=vres is not None),
        grid=(t // ts,),
        in_specs=specs,
        out_specs=[tok] * 7,
        out_shape=out_shape,
        compiler_params=pltpu.CompilerParams(dimension_semantics=("parallel",),
                                             vmem_limit_bytes=VMEM_LIMIT),
        name="rwkv_proj",
    )(*ins)


def _to_scan_kernel(x_ref, o_ref, zt_scr, w_scr, *, gb, d, n):
    heads = d // n
    per_tile = LANES // n
    for b in range(gb):
        for j in range(d // LANES):
            xt = x_ref[b, :, j * LANES:(j + 1) * LANES].T
            for hh in range(per_tile):
                h = j * per_tile + hh
                zt_scr[pl.ds((b * heads + h) * PAIR_PITCH, n), :] = xt[hh * n:(hh + 1) * n]
    tt = o_ref.shape[0]
    for c in range(n):
        w_scr[pl.ds(c * TIME_PITCH, tt), :] = zt_scr[pl.ds(c, gb * heads, stride=PAIR_PITCH), :].T
    for t in range(tt):
        o_ref[t] = w_scr[pl.ds(t, n, stride=TIME_PITCH), :]


def _to_scan(x2d, gb, seq):
    t, d = x2d.shape
    tt = LANES
    n = N_HEAD
    pairs = gb * d // n
    return pl.pallas_call(
        functools.partial(_to_scan_kernel, gb=gb, d=d, n=n),
        grid=(seq // tt,),
        in_specs=[pl.BlockSpec((gb, tt, d), lambda i: (0, i, 0))],
        out_specs=pl.BlockSpec((tt, n, pairs), lambda i: (i, 0, 0)),
        out_shape=jax.ShapeDtypeStruct((seq, n, pairs), F32),
        scratch_shapes=[pltpu.VMEM((pairs * PAIR_PITCH, tt), F32), pltpu.VMEM((n * TIME_PITCH, pairs), F32)],
        compiler_params=pltpu.CompilerParams(dimension_semantics=("parallel",), vmem_limit_bytes=VMEM_LIMIT),
        name="to_scan_layout",
    )(x2d.reshape(gb, seq, d))


def _from_scan_kernel(y_ref, o_ref, zt_scr, w_scr, *, gb, d, n):
    heads = d // n
    tt = y_ref.shape[0]
    for t in range(tt):
        w_scr[pl.ds(t, n, stride=TIME_PITCH), :] = y_ref[t]
    for c in range(n):
        zt_scr[pl.ds(c, gb * heads, stride=PAIR_PITCH), :] = w_scr[pl.ds(c * TIME_PITCH, tt), :].T
    per_tile = LANES // n
    for b in range(gb):
        for j in range(d // LANES):
            xt = jnp.concatenate([zt_scr[pl.ds((b * heads + j * per_tile + hh) * PAIR_PITCH, n), :]
                                  for hh in range(per_tile)], axis=0)
            o_ref[b, :, j * LANES:(j + 1) * LANES] = xt.T


def _from_scan(y, gb, d):
    seq, n, pairs = y.shape
    tt = LANES
    return pl.pallas_call(
        functools.partial(_from_scan_kernel, gb=gb, d=d, n=n),
        grid=(seq // tt,),
        in_specs=[pl.BlockSpec((tt, n, pairs), lambda i: (i, 0, 0))],
        out_specs=pl.BlockSpec((gb, tt, d), lambda i: (0, i, 0)),
        out_shape=jax.ShapeDtypeStruct((gb, seq, d), F32),
        scratch_shapes=[pltpu.VMEM((pairs * PAIR_PITCH, tt), F32), pltpu.VMEM((n * TIME_PITCH, pairs), F32)],
        compiler_params=pltpu.CompilerParams(dimension_semantics=("parallel",), vmem_limit_bytes=VMEM_LIMIT),
        name="from_scan_layout",
    )(y).reshape(gb * seq, d)


def _rows_of_sums(parts):
    row = lax.broadcasted_iota(jnp.int32, parts[0].shape, 0)
    dist = SUBLANES // 2
    while dist >= 1:
        lower = (row & dist) == 0
        half = len(parts) // 2
        parts = [jnp.where(lower, parts[j] + pltpu.roll(parts[j], SUBLANES - dist, 0),
                           parts[j + half] + pltpu.roll(parts[j + half], dist, 0)) for j in range(half)]
        dist //= 2
    return parts[0]


def _wkv_kernel(r_ref, w_ref, k_ref, v_ref, kk_ref, al_ref, rk_ref, lxg_ref, lxb_ref, y_ref, s_scr, *, tc, n):
    @pl.when(pl.program_id(1) == 0)
    def _():
        s_scr[...] = jnp.zeros_like(s_scr)

    groups = n // SUBLANES
    inv_n = 1.0 / n

    def step(t, c_prev):
        k = k_ref[t]
        r = r_ref[t]
        kk = kk_ref[t]
        kk = kk / jnp.maximum(jnp.sqrt(jnp.sum(kk * kk, axis=0, keepdims=True)), 1e-12)
        c_now = c_prev * w_ref[t]
        inv_c = 1.0 / c_now
        a = -kk * c_prev
        b = kk * al_ref[t]
        b_s = b * inv_c
        k_s = k * inv_c
        wr = c_now * r
        br = jnp.sum(b * r, axis=0, keepdims=True)
        kr = jnp.sum(k * r, axis=0, keepdims=True)

        def vblock(vb, c):
            base = pl.multiple_of(vb * SUBLANES, SUBLANES)
            pa, py = [], []
            for j in range(SUBLANES):
                s = s_scr[base + j]
                pa.append(jnp.sum((s * a).reshape(groups, SUBLANES, LANES), axis=0))
                py.append(jnp.sum((s * wr).reshape(groups, SUBLANES, LANES), axis=0))
            sa = _rows_of_sums(pa)
            vv = v_ref[t, pl.ds(base, SUBLANES), :]
            y_ref[t, pl.ds(base, SUBLANES), :] = _rows_of_sums(py) + sa * br + vv * kr
            for j in range(SUBLANES):
                s_scr[base + j] = s_scr[base + j] + sa[j:j + 1] * b_s + vv[j:j + 1] * k_s
            return c

        lax.fori_loop(0, groups, vblock, 0, unroll=WKV_UNROLL)

        y = y_ref[t]
        yc = y - jnp.sum(y, axis=0, keepdims=True) * inv_n
        var = jnp.sum(yc * yc, axis=0, keepdims=True) * inv_n
        bonus = jnp.sum(r * k * rk_ref[...], axis=0, keepdims=True) * v_ref[t]
        y_ref[t] = yc * lax.rsqrt(var + GN_EPS) * lxg_ref[...] + lxb_ref[...] + bonus
        return c_now

    c_end = lax.fori_loop(0, tc, step, jnp.ones((n, LANES), F32))
    s_scr[...] = s_scr[...] * c_end[None]


def _wkv(r, w, k, v, kk, alr, rk, lxg, lxb, tc=16):
    s, n, pairs = r.shape
    spec = pl.BlockSpec((tc, n, LANES), lambda p, c: (c, 0, p))
    pspec = pl.BlockSpec((n, LANES), lambda p, c: (0, p))
    return pl.pallas_call(
        functools.partial(_wkv_kernel, tc=tc, n=n),
        grid=(pairs // LANES, s // tc),
        in_specs=[spec] * 6 + [pspec] * 3,
        out_specs=spec,
        out_shape=jax.ShapeDtypeStruct((s, n, pairs), F32),
        scratch_shapes=[pltpu.VMEM((n, n, LANES), F32)],
        compiler_params=pltpu.CompilerParams(dimension_semantics=("parallel", "arbitrary"),
                                             vmem_limit_bytes=VMEM_LIMIT),
        name="wkv7_scan",
    )(r, w, k, v, kk, alr, rk, lxg, lxb)


def _rwkv_out_kernel(z_ref, g_ref, x_ref, wo_ref, lng_ref, lnb_ref, o_ref, *, alpha):
    h = _bdot(z_ref[...] * g_ref[...], wo_ref[...])
    o_ref[...] = _layer_norm(alpha * x_ref[...] + h, lng_ref[...], lnb_ref[...])


def _rwkv_out(z, g, x2d, w_o, ln_g, ln_b, alpha, ts=512):
    t, d = x2d.shape
    row = lambda a: a.reshape(1, d)
    tok = pl.BlockSpec((ts, d), lambda i: (i, 0))
    consts = [w_o.astype(BF16), row(ln_g), row(ln_b)]
    return pl.pallas_call(
        functools.partial(_rwkv_out_kernel, alpha=alpha),
        grid=(t // ts,),
        in_specs=[tok] * 3 + [_const_spec(a.shape) for a in consts],
        out_specs=tok,
        out_shape=jax.ShapeDtypeStruct((t, d), F32),
        compiler_params=pltpu.CompilerParams(dimension_semantics=("parallel",),
                                             vmem_limit_bytes=VMEM_LIMIT),
        name="rwkv_out",
    )(z, g, x2d, *consts)


def _ffn_kernel(xp_ref, x_ref, wup_ref, cw_ref, cb_ref, wdn_ref, lng_ref, lnb_ref, o_ref, u_scr,
                *, ts, seq, d_ff, fc, alpha):
    first = (pl.program_id(0) * ts) % seq == 0
    x = x_ref[...]
    xp = jnp.where(first, 0.0, xp_ref[...])
    xe = jnp.concatenate([xp, x], axis=0).astype(BF16)

    def conv_cols(off):
        u_scr[...] = jnp.dot(xe, wup_ref[:, off:off + fc], preferred_element_type=F32)
        cw = cw_ref[:, off:off + fc]
        return (u_scr[SUBLANES - 2:SUBLANES - 2 + ts, :] * cw[0:1]
                + u_scr[SUBLANES - 1:SUBLANES - 1 + ts, :] * cw[1:2]
                + u_scr[SUBLANES:SUBLANES + ts, :] * cw[2:3]
                + cb_ref[:, off:off + fc])

    acc = jnp.zeros(x.shape, F32)
    for c in range(d_ff // fc):
        gate = conv_cols(c * fc)
        val = conv_cols(d_ff + c * fc)
        act = jax.nn.silu(gate) * val
        acc = acc + jnp.dot(act.astype(BF16), wdn_ref[c * fc:(c + 1) * fc, :], preferred_element_type=F32)
    o_ref[...] = _layer_norm(alpha * x + acc, lng_ref[...], lnb_ref[...])


def _ffn(x2d, seq, w_up, conv_w, conv_b, w_down, ln_g, ln_b, alpha, ts=512, fc=1408):
    t, d = x2d.shape
    d_ff = w_down.shape[0]
    tok = pl.BlockSpec((ts, d), lambda i: (i, 0))
    single = pl.Buffered(1)
    wup_spec = pl.BlockSpec((d, 2 * d_ff), lambda i: (0, 0), pipeline_mode=single)
    wdn_spec = pl.BlockSpec((d_ff, d), lambda i: (0, 0), pipeline_mode=single)
    consts = [conv_w, conv_b.reshape(1, -1)]
    rows = [ln_g.reshape(1, d), ln_b.reshape(1, d)]
    return pl.pallas_call(
        functools.partial(_ffn_kernel, ts=ts, seq=seq, d_ff=d_ff, fc=fc, alpha=alpha),
        grid=(t // ts,),
        in_specs=[_prev_rows_spec(ts, d), tok, wup_spec] + [_const_spec(a.shape) for a in consts]
                 + [wdn_spec] + [_const_spec(a.shape) for a in rows],
        out_specs=tok,
        out_shape=jax.ShapeDtypeStruct((t, d), F32),
        scratch_shapes=[pltpu.VMEM((ts + SUBLANES, fc), F32)],
        compiler_params=pltpu.CompilerParams(dimension_semantics=("parallel",),
                                             vmem_limit_bytes=VMEM_LIMIT),
        name="conv_ffn",
    )(x2d, x2d, w_up.astype(BF16), *consts, w_down.astype(BF16), *rows)


def _dsa_proj_kernel(x_ref, wcq_ref, wckv_ref, wki_ref, wwit_ref, qg_ref, kvg_ref, wuq_ref, wukp_ref,
                     wqi_ref, kig_ref, kib_ref, qa_o, qi_o, ki_o, wit_o, ckv_o, ckvt_o, *, cw):
    x = x_ref[...].astype(BF16)
    c_q = jnp.dot(x, wcq_ref[...], preferred_element_type=F32)
    c_kv = jnp.dot(x, wckv_ref[...], preferred_element_type=F32)
    k_idx = jnp.dot(x, wki_ref[...], preferred_element_type=F32)
    w_idx_t = _bdot_nt(wwit_ref[...], x)

    def rms(z, g):
        return z * lax.rsqrt(jnp.mean(z * z, axis=-1, keepdims=True) + 1e-6) * g

    c_q = rms(c_q, qg_ref[...]).astype(BF16)
    c_kv = rms(c_kv, kvg_ref[...])
    ckv_o[...] = c_kv.astype(BF16)
    for jc in range(c_kv.shape[0] // cw):
        ckvt_o[jc] = c_kv[jc * cw:(jc + 1) * cw].T.astype(BF16)
    q = jnp.dot(c_q, wuq_ref[...], preferred_element_type=F32)
    qk_scale = QK_HEAD ** -0.5
    for hp in range(ATT_HEADS // 2):
        qa = _bdot(q[:, hp * LANES:(hp + 1) * LANES], wukp_ref[hp]) * qk_scale
        qa_o[2 * hp] = qa[:, :KV_LORA].astype(BF16)
        qa_o[2 * hp + 1] = qa[:, KV_LORA:].astype(BF16)
    for h in range(IDX_HEADS):
        qi_o[h] = jnp.dot(c_q, wqi_ref[h], preferred_element_type=F32).astype(BF16)
    ki_o[...] = _layer_norm(k_idx, kig_ref[...], kib_ref[...]).astype(BF16)
    wit_o[...] = w_idx_t * (IDX_HEADS ** -0.5 * IDX_DIM ** -0.5)


def _dsa_proj(x2d, w_in, q_norm_g, kv_norm_g, w_uq, w_uk, w_qidx, kidx_g, kidx_b, ts=512):
    t, d = x2d.shape
    cw = DSA_ROWS
    o1, o2, o3 = Q_LORA, Q_LORA + KV_LORA, Q_LORA + KV_LORA + IDX_DIM
    w_in = w_in.astype(BF16)
    z = jnp.zeros((ATT_HEADS // 2, QK_HEAD, KV_LORA), F32)
    wuk_pair = jnp.concatenate([jnp.concatenate([w_uk[0::2], z], axis=2),
                                jnp.concatenate([z, w_uk[1::2]], axis=2)], axis=1).astype(BF16)
    wqi = w_qidx.reshape(Q_LORA, IDX_HEADS, IDX_DIM).transpose(1, 0, 2).astype(BF16)
    consts = [w_in[:, :o1], w_in[:, o1:o2], w_in[:, o2:o3], w_in[:, o3:].T, q_norm_g.reshape(1, -1),
              kv_norm_g.reshape(1, -1), w_uq.astype(BF16), wuk_pair, wqi, kidx_g.reshape(1, -1),
              kidx_b.reshape(1, -1)]
    out_shape = [jax.ShapeDtypeStruct((ATT_HEADS, t, KV_LORA), BF16),
                 jax.ShapeDtypeStruct((IDX_HEADS, t, IDX_DIM), BF16),
                 jax.ShapeDtypeStruct((t, IDX_DIM), BF16),
                 jax.ShapeDtypeStruct((IDX_HEADS, t), F32),
                 jax.ShapeDtypeStruct((t, KV_LORA), BF16),
                 jax.ShapeDtypeStruct((t // cw, KV_LORA, cw), BF16)]
    out_specs = [pl.BlockSpec((ATT_HEADS, ts, KV_LORA), lambda i: (0, i, 0)),
                 pl.BlockSpec((IDX_HEADS, ts, IDX_DIM), lambda i: (0, i, 0)),
                 pl.BlockSpec((ts, IDX_DIM), lambda i: (i, 0)),
                 pl.BlockSpec((IDX_HEADS, ts), lambda i: (0, i)),
                 pl.BlockSpec((ts, KV_LORA), lambda i: (i, 0)),
                 pl.BlockSpec((ts // cw, KV_LORA, cw), lambda i: (i, 0, 0))]
    return pl.pallas_call(
        functools.partial(_dsa_proj_kernel, cw=cw),
        grid=(t // ts,),
        in_specs=[pl.BlockSpec((ts, d), lambda i: (i, 0))] + [_const_spec(a.shape) for a in consts],
        out_specs=out_specs,
        out_shape=out_shape,
        compiler_params=pltpu.CompilerParams(dimension_semantics=("parallel",),
                                             vmem_limit_bytes=VMEM_LIMIT),
        name="dsa_proj",
    )(x2d, *consts)


def _bias_tiles_kernel(rb_ref, o_ref, *, qb):
    c = lax.broadcasted_iota(jnp.int32, (2 * qb, qb), 0)
    tl = lax.broadcasted_iota(jnp.int32, (2 * qb, qb), 1)
    n = jnp.maximum(qb + tl - c, 0)
    max_exact = REL_BUCKETS // 2
    nf = jnp.maximum(n, 1).astype(F32)
    large = max_exact + (jnp.log(nf / max_exact) / math.log(REL_MAX_DIST / max_exact)
                         * (REL_BUCKETS - max_exact)).astype(jnp.int32)
    large = jnp.minimum(large, REL_BUCKETS - 1)
    bucket = jnp.where(n < max_exact, n, large)
    for h in range(ATT_HEADS):
        acc = jnp.zeros((2 * qb, qb), F32)
        for bkt in range(REL_BUCKETS):
            acc = jnp.where(bucket == bkt, rb_ref[bkt, h], acc)
        o_ref[h] = acc - rb_ref[REL_BUCKETS - 1, h]


def _bias_tiles(rel_bias, qb):
    return pl.pallas_call(
        functools.partial(_bias_tiles_kernel, qb=qb),
        in_specs=[pl.BlockSpec(memory_space=pltpu.SMEM)],
        out_specs=pl.BlockSpec(memory_space=pltpu.VMEM),
        out_shape=jax.ShapeDtypeStruct((ATT_HEADS, 2 * qb, qb), F32),
        name="rel_bias_tiles",
    )(rel_bias)


def _dsa_attn_kernel(qi_ref, wit_ref, ki_ref, ckv_ref, ckvt_ref, qa_ref, near_ref, tril_ref, wuvt_ref,
                     wo_ref, x_ref, lng_ref, lnb_ref, o_ref, key_scr, khi_scr, klo_scr, madd_scr, m_scr, den_scr, acc_scr,
                     sc_scr, l_scr, p_scr,
                     *, topk, qb, alpha):
    cw = qb
    i = pl.program_id(1)
    nch = i + 1
    t_pos = i * qb + lax.broadcasted_iota(jnp.int32, (1, qb), 1)
    s_loc = lax.broadcasted_iota(jnp.int32, (cw, 1), 0)
    int_min = jnp.int32(INT_MIN)

    qi_all = qi_ref[...].reshape(IDX_HEADS * qb, IDX_DIM)
    wit = wit_ref[...]

    def score_chunk(c, carry):
        off = pl.multiple_of(c * cw, cw)
        s_all = _bdot_nt(ki_ref[pl.ds(off, cw), :], qi_all)
        score = jnp.zeros((cw, qb), F32)
        for h in range(IDX_HEADS):
            score = score + jnp.maximum(s_all[:, h * qb:(h + 1) * qb], 0.0) * wit[h:h + 1]
        bits = lax.bitcast_convert_type(score, jnp.int32)
        key = jnp.where(bits < 0, bits ^ jnp.int32(0x7FFFFFFF), bits)
        key = jnp.where(off + s_loc <= t_pos, key, int_min)
        key_scr[c] = key
        khi_scr[c] = lax.shift_right_arithmetic(key, 16).astype(jnp.int16)
        klo_scr[c] = ((key & 0xFFFF) - 32768).astype(jnp.int16)
        return carry

    lax.fori_loop(0, nch, score_chunk, 0)

    def count(pred):
        def body(c, acc):
            kc = key_scr[c]
            for r in range(cw // SUBLANES):
                acc = acc + jnp.where(pred(kc[r * SUBLANES:(r + 1) * SUBLANES]), 1.0, 0.0)
            return acc

        acc = lax.fori_loop(0, nch, body, jnp.zeros((SUBLANES, qb), F32))
        return jnp.sum(acc, axis=0, keepdims=True)

    rows16 = 2 * SUBLANES
    one16, zero16, min16 = jnp.int16(1), jnp.int16(0), jnp.int16(-32768)

    def count16(scr, pred):
        def body(c, acc):
            kc = scr[c]
            for r in range(cw // rows16):
                acc = acc + jnp.where(pred(kc[r * rows16:(r + 1) * rows16]), one16, zero16)
            return acc

        acc = lax.fori_loop(0, nch, body, jnp.zeros((rows16, qb), jnp.int16))
        return jnp.sum(acc.astype(jnp.int32), axis=0, keepdims=True)

    def search16(scr, want):
        tau = jnp.where(count16(scr, lambda x: x >= zero16) >= want, jnp.int32(0), jnp.int32(-32768))

        def bit_step(bi, tau):
            cand = tau | lax.shift_left(jnp.int32(1), jnp.int32(14) - bi)
            c16 = cand.astype(jnp.int16)
            return jnp.where(count16(scr, lambda x: x >= c16) >= want, cand, tau)

        return lax.fori_loop(0, 15, bit_step, tau)

    tau_hi = search16(khi_scr, jnp.int32(topk))
    th16 = tau_hi.astype(jnp.int16)
    want_lo = topk - count16(khi_scr, lambda x: x > th16)

    def mask_chunk(c, carry):
        klo_scr[c] = jnp.where(khi_scr[c] == th16, klo_scr[c], min16)
        return carry

    lax.fori_loop(0, nch, mask_chunk, 0)
    tau_lo = search16(klo_scr, want_lo)
    tau = lax.shift_left(tau_hi, 16) | ((tau_lo + 32768) & 0xFFFF)
    kf = float(topk)

    need = kf - count(lambda kc: kc > tau)
    has_kth = tau > int_min
    tril = tril_ref[...]

    def select_chunk(c, run):
        kc = key_scr[c]
        eq = jnp.logical_and(kc == tau, has_kth)
        e = jnp.where(eq, 1.0, 0.0)
        rank = jnp.dot(tril, e.astype(BF16), preferred_element_type=F32) + run
        sel = jnp.logical_or(kc > tau, jnp.logical_and(eq, rank <= need))
        madd_scr[c] = jnp.where(sel, 0.0, NEG_MASK)
        return run + jnp.sum(e, axis=0, keepdims=True)

    lax.fori_loop(0, nch, select_chunk, jnp.zeros((1, qb), F32))

    m_scr[...] = jnp.full(m_scr.shape, -jnp.inf, F32)
    den_scr[...] = jnp.zeros(den_scr.shape, F32)
    acc_scr[...] = jnp.zeros(acc_scr.shape, F32)
    qa_all = qa_ref[...].reshape(ATT_HEADS * qb, KV_LORA)
    halves = qb // LANES

    def att_chunk(c, with_bias):
        off = pl.multiple_of(c * cw, cw)
        ckv_c = ckv_ref[pl.ds(off, cw), :]
        boff = pl.multiple_of(jnp.where(c == i, qb, 0), qb)
        l_scr[...] = _bdot_nt(ckv_c, qa_all)
        for g in range(ATT_HEADS * halves):
            cols = slice(g * LANES, (g + 1) * LANES)
            qcols = slice((g % halves) * LANES, (g % halves + 1) * LANES)
            l = l_scr[:, cols] + madd_scr[c, :, qcols]
            if with_bias:
                l = l + near_ref[g // halves, pl.ds(boff, qb), qcols]
            m_old = m_scr[:, cols]
            m_new = jnp.maximum(m_old, jnp.max(l, axis=0, keepdims=True))
            p = jnp.exp(l - m_new)
            scale = jnp.exp(m_old - m_new)
            den_scr[:, cols] = den_scr[:, cols] * scale + jnp.sum(p, axis=0, keepdims=True)
            m_scr[:, cols] = m_new
            sc_scr[:, cols] = scale
            p_scr[:, cols] = p.astype(BF16)
        pv = jnp.dot(ckvt_ref[c], p_scr[...], preferred_element_type=F32)
        acc_scr[...] = acc_scr[...] * sc_scr[...] + pv

    def far_body(c, carry):
        att_chunk(c, False)
        return carry

    def near_body(c, carry):
        att_chunk(c, True)
        return carry

    c_near = jnp.maximum(i - 1, 0)
    lax.fori_loop(0, c_near, far_body, 0)
    lax.fori_loop(c_near, nch, near_body, 0)

    o_rows = []
    for h in range(ATT_HEADS):
        o_lat_t = acc_scr[:, h * qb:(h + 1) * qb] / den_scr[:, h * qb:(h + 1) * qb]
        o_rows.append(_bdot(wuvt_ref[h], o_lat_t))
    h_out = _bdot(jnp.concatenate(o_rows, axis=0).T, wo_ref[...])
    o_ref[...] = _layer_norm(alpha * x_ref[...] + h_out, lng_ref[...], lnb_ref[...])


def _dsa_attn(x2d, batch, seq, qa, qi, ki, wit, ckv, ckvt, near, w_uv, w_o, ln_g, ln_b, alpha):
    t, d = x2d.shape
    qb = DSA_ROWS
    assert KV_LORA == LANES
    nb = seq // qb
    topk = min(TOPK_MAX, seq // 4)
    tril = (jnp.arange(qb)[:, None] >= jnp.arange(qb)[None, :]).astype(BF16)
    wuv_t = w_uv.transpose(0, 2, 1).astype(BF16)
    wo = w_o.astype(BF16)
    tok = lambda b, i: (b * nb + i, 0)
    in_specs = [
        pl.BlockSpec((IDX_HEADS, qb, IDX_DIM), lambda b, i: (0, b * nb + i, 0)),
        pl.BlockSpec((IDX_HEADS, qb), lambda b, i: (0, b * nb + i)),
        pl.BlockSpec((seq, IDX_DIM), lambda b, i: (b, 0)),
        pl.BlockSpec((seq, KV_LORA), lambda b, i: (b, 0)),
        pl.BlockSpec((nb, KV_LORA, qb), lambda b, i: (b, 0, 0)),
        pl.BlockSpec((ATT_HEADS, qb, KV_LORA), lambda b, i: (0, b * nb + i, 0)),
        pl.BlockSpec(near.shape, lambda b, i: (0, 0, 0), pipeline_mode=pl.Buffered(1)),
        _const_spec(tril.shape),
        _const_spec(wuv_t.shape),
        _const_spec(wo.shape),
        pl.BlockSpec((qb, d), tok),
        _const_spec((1, d)),
        _const_spec((1, d)),
    ]
    scratch = [pltpu.VMEM((nb, qb, qb), jnp.int32),
               pltpu.VMEM((nb, qb, qb), jnp.int16),
               pltpu.VMEM((nb, qb, qb), jnp.int16),
               pltpu.VMEM((nb, qb, qb), F32),
               pltpu.VMEM((1, ATT_HEADS * qb), F32),
               pltpu.VMEM((1, ATT_HEADS * qb), F32),
               pltpu.VMEM((KV_LORA, ATT_HEADS * qb), F32),
               pltpu.VMEM((1, ATT_HEADS * qb), F32),
               pltpu.VMEM((qb, ATT_HEADS * qb), F32),
               pltpu.VMEM((qb, ATT_HEADS * qb), BF16)]
    return pl.pallas_call(
        functools.partial(_dsa_attn_kernel, topk=topk, qb=qb, alpha=alpha),
        grid=(batch, nb),
        in_specs=in_specs,
        out_specs=pl.BlockSpec((qb, d), tok),
        out_shape=jax.ShapeDtypeStruct((t, d), F32),
        scratch_shapes=scratch,
        compiler_params=pltpu.CompilerParams(dimension_semantics=("parallel", "parallel"),
                                             vmem_limit_bytes=VMEM_LIMIT),
        name="dsa_attn",
    )(qi, wit, ki, ckv, ckvt, qa, near, tril, wuv_t, wo, x2d, ln_g.reshape(1, d), ln_b.reshape(1, d))


def kernel(x, ln_g, ln_b, rwkv_mix, rwkv_w_rkv, rwkv_w0, rwkv_w1, rwkv_w2, rwkv_a0, rwkv_a1, rwkv_a2, rwkv_v0, rwkv_v1, rwkv_v2, rwkv_g1, rwkv_g2, rwkv_k_k, rwkv_k_a, rwkv_r_k, rwkv_lnx_g, rwkv_lnx_b, rwkv_w_o, dsa_w_in, dsa_q_norm_g, dsa_kv_norm_g, dsa_w_uq, dsa_w_uk, dsa_w_uv, dsa_w_qidx, dsa_kidx_g, dsa_kidx_b, dsa_w_o, rel_bias, ffn_w_up, ffn_conv_w, ffn_conv_b, ffn_w_down):
    batch, seq, d = x.shape
    depth = ln_g.shape[0]
    heads = d // N_HEAD
    alpha = (2 * depth) ** 0.25
    gb = LANES // heads
    groups = batch // gb
    t = gb * seq
    xs = [x[g * gb:(g + 1) * gb].reshape(t, d) for g in range(groups)]

    def scan_param(v):
        return jnp.tile(v.reshape(heads, N_HEAD).T, (1, gb))

    near = _bias_tiles(rel_bias, DSA_ROWS)

    to_scan = functools.partial(_to_scan, gb=gb, seq=seq)

    v_first = [None] * groups
    for i in range(depth):
        j = i // 2
        for g in range(groups):
            x2d = xs[g]
            if i % 2 == 0:
                vres = None if j == 0 else (v_first[g], rwkv_v0[j - 1], rwkv_v1[j - 1], rwkv_v2[j - 1])
                r, w, k, v, kk, lr, gate = _rwkv_proj(
                    x2d, seq, rwkv_mix[j], rwkv_w_rkv[j], rwkv_w0[j], rwkv_w1[j], rwkv_w2[j], rwkv_a0[j],
                    rwkv_a1[j], rwkv_a2[j], rwkv_g1[j], rwkv_g2[j], rwkv_k_k[j], rwkv_k_a[j], vres)
                if j == 0:
                    v_first[g] = v
                z = _wkv(to_scan(r), to_scan(w), to_scan(k), to_scan(v), to_scan(kk), to_scan(lr),
                         scan_param(rwkv_r_k[j].reshape(-1)), scan_param(rwkv_lnx_g[j]), scan_param(rwkv_lnx_b[j]))
                x2d = _rwkv_out(_from_scan(z, gb, d), gate, x2d, rwkv_w_o[j], ln_g[i, 0], ln_b[i, 0], alpha)
            else:
                qa, qi, ki, wit, ckv, ckvt = _dsa_proj(x2d, dsa_w_in[j], dsa_q_norm_g[j], dsa_kv_norm_g[j],
                                                       dsa_w_uq[j], dsa_w_uk[j], dsa_w_qidx[j], dsa_kidx_g[j],
                                                       dsa_kidx_b[j])
                x2d = _dsa_attn(x2d, gb, seq, qa, qi, ki, wit, ckv, ckvt, near, dsa_w_uv[j], dsa_w_o[j],
                                ln_g[i, 0], ln_b[i, 0], alpha)
            xs[g] = _ffn(x2d, seq, ffn_w_up[i], ffn_conv_w[i], ffn_conv_b[i], ffn_w_down[i],
                         ln_g[i, 1], ln_b[i, 1], alpha)
    return jnp.concatenate([z.reshape(gb, seq, d) for z in xs], axis=0)
```

```python
import functools
import math

import jax
import jax.numpy as jnp
from jax import lax
from jax.experimental import pallas as pl
from jax.experimental.pallas import tpu as pltpu

F32 = jnp.float32
BF16 = jnp.bfloat16

N_HEAD = 64
ATT_HEADS = 16
QK_HEAD = 64
V_HEAD = 64
Q_LORA = 256
KV_LORA = 128
IDX_HEADS = 8
IDX_DIM = 64
TOPK_MAX = 256
DSA_ROWS = 256
REL_BUCKETS = 32
REL_MAX_DIST = 128
CONV_W = 3
GN_EPS = 64e-5
LN_EPS = 1e-5

LANES = 128
SUBLANES = 8
VMEM_LIMIT = 56 * 1024 * 1024
PAIR_PITCH = N_HEAD + 8
TIME_PITCH = LANES + 8
WKV_STEPS = 32
WKV_UNROLL = 8

NEG_MASK = -1e30
INT_MIN = -2 ** 31


def _bdot(a, b):
    return jnp.dot(a.astype(BF16), b.astype(BF16), preferred_element_type=F32)


def _bdot_nt(a, b):
    return lax.dot_general(a.astype(BF16), b.astype(BF16), (((1,), (1,)), ((), ())),
                           preferred_element_type=F32)


def _layer_norm(x, g, b):
    mu = jnp.mean(x, axis=-1, keepdims=True)
    xc = x - mu
    var = jnp.mean(xc * xc, axis=-1, keepdims=True)
    return xc * lax.rsqrt(var + LN_EPS) * g + b


def _const_spec(shape):
    nd = len(shape)
    return pl.BlockSpec(shape, lambda *_: (0,) * nd)


def _prev_rows_spec(ts, d):
    return pl.BlockSpec((SUBLANES, d), lambda i: (jnp.maximum(i * (ts // SUBLANES) - 1, 0), 0))


def _rwkv_proj_kernel(*refs, ts, seq, has_vres):
    (xp_ref, x_ref, mix_ref, wr_ref, wk_ref, wv_ref, w0_ref, w1_ref, w2_ref, a0_ref, a1_ref,
     a2_ref, g1_ref, g2_ref) = refs[:14]
    if has_vres:
        vf_ref, v0_ref, v1_ref, v2_ref = refs[14:18]
        outs = refs[18:]
    else:
        outs = refs[14:]
    r_o, w_o, k_o, v_o, a_o, g_o = outs

    first = (pl.program_id(0) * ts) % seq == 0
    x = x_ref[...]
    p_last = jnp.where(first, 0.0, xp_ref[SUBLANES - 1:SUBLANES, :])
    rows = lax.broadcasted_iota(jnp.int32, (ts, 1), 0)
    x_shift = jnp.where(rows == 0, p_last, pltpu.roll(x, 1, 0))
    xx = x_shift - x
    mix = mix_ref[...]

    def xm(j):
        return x + xx * mix[j:j + 1]

    r = _bdot(xm(0), wr_ref[...])
    k = _bdot(xm(1), wk_ref[...])
    xv = xm(2)
    v = _bdot(xv, wv_ref[...])
    w_log = -jax.nn.softplus(-(w0_ref[...] + _bdot(jnp.tanh(_bdot(xm(3), w1_ref[...])), w2_ref[...]))) - 0.5
    decay = jnp.exp(-jnp.exp(w_log))
    if has_vres:
        v = v + (vf_ref[...] - v) * jax.nn.sigmoid(v0_ref[...] + _bdot(_bdot(xv, v1_ref[...]), v2_ref[...]))
    a = jax.nn.sigmoid(a0_ref[...] + _bdot(_bdot(xm(4), a1_ref[...]), a2_ref[...]))
    g = _bdot(jax.nn.sigmoid(_bdot(xm(5), g1_ref[...])), g2_ref[...])
    r_o[...] = r
    w_o[...] = decay
    k_o[...] = k
    v_o[...] = v
    a_o[...] = a
    g_o[...] = g


def _rwkv_proj(x2d, seq, mix, w_rkv, w0, w1, w2, a0, a1, a2, g1, g2, vres, ts=256):
    t, d = x2d.shape
    row = lambda v: v.reshape(1, d)
    tok = pl.BlockSpec((ts, d), lambda i: (i, 0))
    ins = [x2d, x2d, mix, w_rkv[0].astype(BF16), w_rkv[1].astype(BF16), w_rkv[2].astype(BF16),
           row(w0), w1.astype(BF16), w2.astype(BF16), row(a0), a1.astype(BF16), a2.astype(BF16),
           g1.astype(BF16), g2.astype(BF16)]
    specs = [_prev_rows_spec(ts, d), tok] + [_const_spec(a.shape) for a in ins[2:]]
    if vres is not None:
        v_first, v0, v1, v2 = vres
        extra = [v_first, row(v0), v1.astype(BF16), v2.astype(BF16)]
        ins += extra
        specs += [tok] + [_const_spec(a.shape) for a in extra[1:]]
    out_shape = [jax.ShapeDtypeStruct((t, d), F32)] * 6
    return pl.pallas_call(
        functools.partial(_rwkv_proj_kernel, ts=ts, seq=seq, has_vres=vres is not None),
        grid=(t // ts,),
        in_specs=specs,
        out_specs=[tok] * 6,
        out_shape=out_shape,
        compiler_params=pltpu.CompilerParams(dimension_semantics=("parallel",),
                                             vmem_limit_bytes=VMEM_LIMIT),
        name="rwkv_proj",
    )(*ins)


def _to_scan_kernel(x_ref, o_ref, zt_scr, w_scr, *, gb, d, n):
    heads = d // n
    per_tile = LANES // n
    for b in range(gb):
        for j in range(d // LANES):
            xt = x_ref[b, :, j * LANES:(j + 1) * LANES].T
            for hh in range(per_tile):
                h = j * per_tile + hh
                zt_scr[pl.ds((b * heads + h) * PAIR_PITCH, n), :] = xt[hh * n:(hh + 1) * n]
    tt = o_ref.shape[0]
    for c in range(n):
        w_scr[pl.ds(c * TIME_PITCH, tt), :] = zt_scr[pl.ds(c, gb * heads, stride=PAIR_PITCH), :].T
    for t in range(tt):
        o_ref[t] = w_scr[pl.ds(t, n, stride=TIME_PITCH), :]


def _to_scan(x2d, gb, seq):
    t, d = x2d.shape
    tt = LANES
    n = N_HEAD
    pairs = gb * d // n
    return pl.pallas_call(
        functools.partial(_to_scan_kernel, gb=gb, d=d, n=n),
        grid=(seq // tt,),
        in_specs=[pl.BlockSpec((gb, tt, d), lambda i: (0, i, 0))],
        out_specs=pl.BlockSpec((tt, n, pairs), lambda i: (i, 0, 0)),
        out_shape=jax.ShapeDtypeStruct((seq, n, pairs), F32),
        scratch_shapes=[pltpu.VMEM((pairs * PAIR_PITCH, tt), F32), pltpu.VMEM((n * TIME_PITCH, pairs), F32)],
        compiler_params=pltpu.CompilerParams(dimension_semantics=("parallel",), vmem_limit_bytes=VMEM_LIMIT),
        name="to_scan_layout",
    )(x2d.reshape(gb, seq, d))


def _from_scan_kernel(y_ref, o_ref, zt_scr, w_scr, *, gb, d, n):
    heads = d // n
    tt = y_ref.shape[0]
    for t in range(tt):
        w_scr[pl.ds(t, n, stride=TIME_PITCH), :] = y_ref[t]
    for c in range(n):
        zt_scr[pl.ds(c, gb * heads, stride=PAIR_PITCH), :] = w_scr[pl.ds(c * TIME_PITCH, tt), :].T
    per_tile = LANES // n
    for b in range(gb):
        for j in range(d // LANES):
            xt = jnp.concatenate([zt_scr[pl.ds((b * heads + j * per_tile + hh) * PAIR_PITCH, n), :]
                                  for hh in range(per_tile)], axis=0)
            o_ref[b, :, j * LANES:(j + 1) * LANES] = xt.T


def _from_scan(y, gb, d):
    seq, n, pairs = y.shape
    tt = LANES
    return pl.pallas_call(
        functools.partial(_from_scan_kernel, gb=gb, d=d, n=n),
        grid=(seq // tt,),
        in_specs=[pl.BlockSpec((tt, n, pairs), lambda i: (i, 0, 0))],
        out_specs=pl.BlockSpec((gb, tt, d), lambda i: (0, i, 0)),
        out_shape=jax.ShapeDtypeStruct((gb, seq, d), F32),
        scratch_shapes=[pltpu.VMEM((pairs * PAIR_PITCH, tt), F32), pltpu.VMEM((n * TIME_PITCH, pairs), F32)],
        compiler_params=pltpu.CompilerParams(dimension_semantics=("parallel",), vmem_limit_bytes=VMEM_LIMIT),
        name="from_scan_layout",
    )(y).reshape(gb * seq, d)


def _rows_of_sums(parts):
    row = lax.broadcasted_iota(jnp.int32, parts[0].shape, 0)
    dist = SUBLANES // 2
    while dist >= 1:
        lower = (row & dist) == 0
        half = len(parts) // 2
        parts = [jnp.where(lower, parts[j] + pltpu.roll(parts[j], SUBLANES - dist, 0),
                           parts[j + half] + pltpu.roll(parts[j + half], dist, 0)) for j in range(half)]
        dist //= 2
    return parts[0]


def _wkv_kernel(r_ref, w_ref, k_ref, v_ref, al_ref, kkp_ref, kap_ref, rk_ref, lxg_ref, lxb_ref, y_ref, s_scr,
                *, tc, n):
    @pl.when(pl.program_id(1) == 0)
    def _():
        s_scr[...] = jnp.zeros_like(s_scr)

    groups = n // SUBLANES
    inv_n = 1.0 / n

    def step(t, c_prev):
        k = k_ref[t]
        r = r_ref[t]
        lr = al_ref[t]
        kk = k * kkp_ref[...]
        kk = kk / jnp.maximum(jnp.sqrt(jnp.sum(kk * kk, axis=0, keepdims=True)), 1e-12)
        k = k * (1.0 + (lr - 1.0) * kap_ref[...])
        c_now = c_prev * w_ref[t]
        inv_c = 1.0 / c_now
        a = -kk * c_prev
        b = kk * lr
        b_s = b * inv_c
        k_s = k * inv_c
        wr = c_now * r
        br = jnp.sum(b * r, axis=0, keepdims=True)
        kr = jnp.sum(k * r, axis=0, keepdims=True)

        def vblock(vb, c):
            base = pl.multiple_of(vb * SUBLANES, SUBLANES)
            pa, py = [], []
            for j in range(SUBLANES):
                s = s_scr[base + j]
                pa.append(jnp.sum((s * a).reshape(groups, SUBLANES, LANES), axis=0))
                py.append(jnp.sum((s * wr).reshape(groups, SUBLANES, LANES), axis=0))
            sa = _rows_of_sums(pa)
            vv = v_ref[t, pl.ds(base, SUBLANES), :]
            y_ref[t, pl.ds(base, SUBLANES), :] = _rows_of_sums(py) + sa * br + vv * kr
            for j in range(SUBLANES):
                s_scr[base + j] = s_scr[base + j] + sa[j:j + 1] * b_s + vv[j:j + 1] * k_s
            return c

        lax.fori_loop(0, groups, vblock, 0, unroll=WKV_UNROLL)

        y = y_ref[t]
        yc = y - jnp.sum(y, axis=0, keepdims=True) * inv_n
        var = jnp.sum(yc * yc, axis=0, keepdims=True) * inv_n
        bonus = jnp.sum(r * k * rk_ref[...], axis=0, keepdims=True) * v_ref[t]
        y_ref[t] = yc * lax.rsqrt(var + GN_EPS) * lxg_ref[...] + lxb_ref[...] + bonus
        return c_now

    c_end = lax.fori_loop(0, tc, step, jnp.ones((n, LANES), F32))
    s_scr[...] = s_scr[...] * c_end[None]


def _wkv(r, w, k, v, lr, params, tc=WKV_STEPS):
    s, n, pairs = r.shape
    spec = pl.BlockSpec((tc, n, LANES), lambda p, c: (c, 0, p))
    pspec = pl.BlockSpec((n, LANES), lambda p, c: (0, p))
    return pl.pallas_call(
        functools.partial(_wkv_kernel, tc=tc, n=n),
        grid=(pairs // LANES, s // tc),
        in_specs=[spec] * 5 + [pspec] * len(params),
        out_specs=spec,
        out_shape=jax.ShapeDtypeStruct((s, n, pairs), F32),
        scratch_shapes=[pltpu.VMEM((n, n, LANES), F32)],
        compiler_params=pltpu.CompilerParams(dimension_semantics=("parallel", "arbitrary"),
                                             vmem_limit_bytes=VMEM_LIMIT),
        name="wkv7_scan",
    )(r, w, k, v, lr, *params)


def _rwkv_out_kernel(z_ref, g_ref, x_ref, wo_ref, lng_ref, lnb_ref, o_ref, *, alpha):
    h = _bdot(z_ref[...] * g_ref[...], wo_ref[...])
    o_ref[...] = _layer_norm(alpha * x_ref[...] + h, lng_ref[...], lnb_ref[...])


def _rwkv_out(z, g, x2d, w_o, ln_g, ln_b, alpha, ts=512):
    t, d = x2d.shape
    row = lambda a: a.reshape(1, d)
    tok = pl.BlockSpec((ts, d), lambda i: (i, 0))
    consts = [w_o.astype(BF16), row(ln_g), row(ln_b)]
    return pl.pallas_call(
        functools.partial(_rwkv_out_kernel, alpha=alpha),
        grid=(t // ts,),
        in_specs=[tok] * 3 + [_const_spec(a.shape) for a in consts],
        out_specs=tok,
        out_shape=jax.ShapeDtypeStruct((t, d), F32),
        compiler_params=pltpu.CompilerParams(dimension_semantics=("parallel",),
                                             vmem_limit_bytes=VMEM_LIMIT),
        name="rwkv_out",
    )(z, g, x2d, *consts)


def _ffn_kernel(xp_ref, x_ref, wup_ref, cw_ref, cb_ref, wdn_ref, lng_ref, lnb_ref, o_ref, u_scr,
                *, ts, seq, d_ff, fc, alpha):
    first = (pl.program_id(0) * ts) % seq == 0
    x = x_ref[...]
    xp = jnp.where(first, 0.0, xp_ref[...])
    xe = jnp.concatenate([xp, x], axis=0).astype(BF16)

    def conv_cols(off):
        u_scr[...] = jnp.dot(xe, wup_ref[:, off:off + fc], preferred_element_type=F32)
        cw = cw_ref[:, off:off + fc]
        return (u_scr[SUBLANES - 2:SUBLANES - 2 + ts, :] * cw[0:1]
                + u_scr[SUBLANES - 1:SUBLANES - 1 + ts, :] * cw[1:2]
                + u_scr[SUBLANES:SUBLANES + ts, :] * cw[2:3]
                + cb_ref[:, off:off + fc])

    acc = jnp.zeros(x.shape, F32)
    for c in range(d_ff // fc):
        gate = conv_cols(c * fc)
        val = conv_cols(d_ff + c * fc)
        act = jax.nn.silu(gate) * val
        acc = acc + jnp.dot(act.astype(BF16), wdn_ref[c * fc:(c + 1) * fc, :], preferred_element_type=F32)
    o_ref[...] = _layer_norm(alpha * x + acc, lng_ref[...], lnb_ref[...])


def _ffn(x2d, seq, w_up, conv_w, conv_b, w_down, ln_g, ln_b, alpha, ts=512, fc=1408):
    t, d = x2d.shape
    d_ff = w_down.shape[0]
    tok = pl.BlockSpec((ts, d), lambda i: (i, 0))
    single = pl.Buffered(1)
    wup_spec = pl.BlockSpec((d, 2 * d_ff), lambda i: (0, 0), pipeline_mode=single)
    wdn_spec = pl.BlockSpec((d_ff, d), lambda i: (0, 0), pipeline_mode=single)
    consts = [conv_w, conv_b.reshape(1, -1)]
    rows = [ln_g.reshape(1, d), ln_b.reshape(1, d)]
    return pl.pallas_call(
        functools.partial(_ffn_kernel, ts=ts, seq=seq, d_ff=d_ff, fc=fc, alpha=alpha),
        grid=(t // ts,),
        in_specs=[_prev_rows_spec(ts, d), tok, wup_spec] + [_const_spec(a.shape) for a in consts]
                 + [wdn_spec] + [_const_spec(a.shape) for a in rows],
        out_specs=tok,
        out_shape=jax.ShapeDtypeStruct((t, d), F32),
        scratch_shapes=[pltpu.VMEM((ts + SUBLANES, fc), F32)],
        compiler_params=pltpu.CompilerParams(dimension_semantics=("parallel",),
                                             vmem_limit_bytes=VMEM_LIMIT),
        name="conv_ffn",
    )(x2d, x2d, w_up.astype(BF16), *consts, w_down.astype(BF16), *rows)


def _dsa_proj_kernel(x_ref, wcq_ref, wckv_ref, wki_ref, wwit_ref, qg_ref, kvg_ref, wuq_ref, wukp_ref,
                     wqi_ref, kig_ref, kib_ref, qa_o, qi_o, ki_o, wit_o, ckv_o, ckvt_o, *, cw):
    x = x_ref[...].astype(BF16)
    c_q = jnp.dot(x, wcq_ref[...], preferred_element_type=F32)
    c_kv = jnp.dot(x, wckv_ref[...], preferred_element_type=F32)
    k_idx = jnp.dot(x, wki_ref[...], preferred_element_type=F32)
    w_idx_t = _bdot_nt(wwit_ref[...], x)

    def rms(z, g):
        return z * lax.rsqrt(jnp.mean(z * z, axis=-1, keepdims=True) + 1e-6) * g

    c_q = rms(c_q, qg_ref[...]).astype(BF16)
    c_kv = rms(c_kv, kvg_ref[...])
    ckv_o[...] = c_kv.astype(BF16)
    for jc in range(c_kv.shape[0] // cw):
        ckvt_o[jc] = c_kv[jc * cw:(jc + 1) * cw].T.astype(BF16)
    q = jnp.dot(c_q, wuq_ref[...], preferred_element_type=F32)
    qk_scale = QK_HEAD ** -0.5
    for hp in range(ATT_HEADS // 2):
        qa = _bdot(q[:, hp * LANES:(hp + 1) * LANES], wukp_ref[hp]) * qk_scale
        qa_o[2 * hp] = qa[:, :KV_LORA].astype(BF16)
        qa_o[2 * hp + 1] = qa[:, KV_LORA:].astype(BF16)
    for h in range(IDX_HEADS):
        qi_o[h] = jnp.dot(c_q, wqi_ref[h], preferred_element_type=F32).astype(BF16)
    ki_o[...] = _layer_norm(k_idx, kig_ref[...], kib_ref[...]).astype(BF16)
    wit_o[...] = w_idx_t * (IDX_HEADS ** -0.5 * IDX_DIM ** -0.5)


def _dsa_proj(x2d, w_in, q_norm_g, kv_norm_g, w_uq, w_uk, w_qidx, kidx_g, kidx_b, ts=512):
    t, d = x2d.shape
    cw = DSA_ROWS
    o1, o2, o3 = Q_LORA, Q_LORA + KV_LORA, Q_LORA + KV_LORA + IDX_DIM
    w_in = w_in.astype(BF16)
    z = jnp.zeros((ATT_HEADS // 2, QK_HEAD, KV_LORA), F32)
    wuk_pair = jnp.concatenate([jnp.concatenate([w_uk[0::2], z], axis=2),
                                jnp.concatenate([z, w_uk[1::2]], axis=2)], axis=1).astype(BF16)
    wqi = w_qidx.reshape(Q_LORA, IDX_HEADS, IDX_DIM).transpose(1, 0, 2).astype(BF16)
    consts = [w_in[:, :o1], w_in[:, o1:o2], w_in[:, o2:o3], w_in[:, o3:].T, q_norm_g.reshape(1, -1),
              kv_norm_g.reshape(1, -1), w_uq.astype(BF16), wuk_pair, wqi, kidx_g.reshape(1, -1),
              kidx_b.reshape(1, -1)]
    out_shape = [jax.ShapeDtypeStruct((ATT_HEADS, t, KV_LORA), BF16),
                 jax.ShapeDtypeStruct((IDX_HEADS, t, IDX_DIM), BF16),
                 jax.ShapeDtypeStruct((t, IDX_DIM), BF16),
                 jax.ShapeDtypeStruct((IDX_HEADS, t), F32),
                 jax.ShapeDtypeStruct((t, KV_LORA), BF16),
                 jax.ShapeDtypeStruct((t // cw, KV_LORA, cw), BF16)]
    out_specs = [pl.BlockSpec((ATT_HEADS, ts, KV_LORA), lambda i: (0, i, 0)),
                 pl.BlockSpec((IDX_HEADS, ts, IDX_DIM), lambda i: (0, i, 0)),
                 pl.BlockSpec((ts, IDX_DIM), lambda i: (i, 0)),
                 pl.BlockSpec((IDX_HEADS, ts), lambda i: (0, i)),
                 pl.BlockSpec((ts, KV_LORA), lambda i: (i, 0)),
                 pl.BlockSpec((ts // cw, KV_LORA, cw), lambda i: (i, 0, 0))]
    return pl.pallas_call(
        functools.partial(_dsa_proj_kernel, cw=cw),
        grid=(t // ts,),
        in_specs=[pl.BlockSpec((ts, d), lambda i: (i, 0))] + [_const_spec(a.shape) for a in consts],
        out_specs=out_specs,
        out_shape=out_shape,
        compiler_params=pltpu.CompilerParams(dimension_semantics=("parallel",),
                                             vmem_limit_bytes=VMEM_LIMIT),
        name="dsa_proj",
    )(x2d, *consts)


def _bias_tiles_kernel(rb_ref, o_ref, *, qb):
    c = lax.broadcasted_iota(jnp.int32, (2 * qb, qb), 0)
    tl = lax.broadcasted_iota(jnp.int32, (2 * qb, qb), 1)
    n = jnp.maximum(qb + tl - c, 0)
    max_exact = REL_BUCKETS // 2
    nf = jnp.maximum(n, 1).astype(F32)
    large = max_exact + (jnp.log(nf / max_exact) / math.log(REL_MAX_DIST / max_exact)
                         * (REL_BUCKETS - max_exact)).astype(jnp.int32)
    large = jnp.minimum(large, REL_BUCKETS - 1)
    bucket = jnp.where(n < max_exact, n, large)
    for h in range(ATT_HEADS):
        acc = jnp.zeros((2 * qb, qb), F32)
        for bkt in range(REL_BUCKETS):
            acc = jnp.where(bucket == bkt, rb_ref[bkt, h], acc)
        o_ref[h] = acc - rb_ref[REL_BUCKETS - 1, h]


def _bias_tiles(rel_bias, qb):
    return pl.pallas_call(
        functools.partial(_bias_tiles_kernel, qb=qb),
        in_specs=[pl.BlockSpec(memory_space=pltpu.SMEM)],
        out_specs=pl.BlockSpec(memory_space=pltpu.VMEM),
        out_shape=jax.ShapeDtypeStruct((ATT_HEADS, 2 * qb, qb), F32),
        name="rel_bias_tiles",
    )(rel_bias)


def _dsa_attn_kernel(qi_ref, wit_ref, ki_ref, ckv_ref, ckvt_ref, qa_ref, near_ref, tril_ref, wuvt_ref,
                     wo_ref, x_ref, lng_ref, lnb_ref, o_ref, key_scr, khi_scr, klo_scr, madd_scr, m_scr, den_scr, acc_scr,
                     sc_scr, l_scr, p_scr,
                     *, topk, qb, alpha):
    cw = qb
    i = pl.program_id(1)
    nch = i + 1
    t_pos = i * qb + lax.broadcasted_iota(jnp.int32, (1, qb), 1)
    s_loc = lax.broadcasted_iota(jnp.int32, (cw, 1), 0)
    int_min = jnp.int32(INT_MIN)

    qi_all = qi_ref[...].reshape(IDX_HEADS * qb, IDX_DIM)
    wit = wit_ref[...]

    def score_chunk(c, carry):
        off = pl.multiple_of(c * cw, cw)
        s_all = _bdot_nt(ki_ref[pl.ds(off, cw), :], qi_all)
        score = jnp.zeros((cw, qb), F32)
        for h in range(IDX_HEADS):
            score = score + jnp.maximum(s_all[:, h * qb:(h + 1) * qb], 0.0) * wit[h:h + 1]
        bits = lax.bitcast_convert_type(score, jnp.int32)
        key = jnp.where(bits < 0, bits ^ jnp.int32(0x7FFFFFFF), bits)
        key = jnp.where(off + s_loc <= t_pos, key, int_min)
        key_scr[c] = key
        khi_scr[c] = lax.shift_right_arithmetic(key, 16).astype(jnp.int16)
        klo_scr[c] = ((key & 0xFFFF) - 32768).astype(jnp.int16)
        return carry

    lax.fori_loop(0, nch, score_chunk, 0)

    def count(pred):
        def body(c, acc):
            kc = key_scr[c]
            for r in range(cw // SUBLANES):
                acc = acc + jnp.where(pred(kc[r * SUBLANES:(r + 1) * SUBLANES]), 1.0, 0.0)
            return acc

        acc = lax.fori_loop(0, nch, body, jnp.zeros((SUBLANES, qb), F32))
        return jnp.sum(acc, axis=0, keepdims=True)

    rows16 = 2 * SUBLANES
    one16, zero16, min16 = jnp.int16(1), jnp.int16(0), jnp.int16(-32768)

    def count16(scr, pred):
        def body(c, acc):
            kc = scr[c]
            for r in range(cw // rows16):
                acc = acc + jnp.where(pred(kc[r * rows16:(r + 1) * rows16]), one16, zero16)
            return acc

        acc = lax.fori_loop(0, nch, body, jnp.zeros((rows16, qb), jnp.int16))
        return jnp.sum(acc.astype(jnp.int32), axis=0, keepdims=True)

    def search16(scr, want):
        tau = jnp.where(count16(scr, lambda x: x >= zero16) >= want, jnp.int32(0), jnp.int32(-32768))

        def bit_step(bi, tau):
            cand = tau | lax.shift_left(jnp.int32(1), jnp.int32(14) - bi)
            c16 = cand.astype(jnp.int16)
            return jnp.where(count16(scr, lambda x: x >= c16) >= want, cand, tau)

        return lax.fori_loop(0, 15, bit_step, tau)

    tau_hi = search16(khi_scr, jnp.int32(topk))
    th16 = tau_hi.astype(jnp.int16)
    want_lo = topk - count16(khi_scr, lambda x: x > th16)

    def mask_chunk(c, carry):
        klo_scr[c] = jnp.where(khi_scr[c] == th16, klo_scr[c], min16)
        return carry

    lax.fori_loop(0, nch, mask_chunk, 0)
    tau_lo = search16(klo_scr, want_lo)
    tau = lax.shift_left(tau_hi, 16) | ((tau_lo + 32768) & 0xFFFF)
    kf = float(topk)

    need = kf - count(lambda kc: kc > tau)
    has_kth = tau > int_min
    tril = tril_ref[...]

    def select_chunk(c, run):
        kc = key_scr[c]
        eq = jnp.logical_and(kc == tau, has_kth)
        e = jnp.where(eq, 1.0, 0.0)
        rank = jnp.dot(tril, e.astype(BF16), preferred_element_type=F32) + run
        sel = jnp.logical_or(kc > tau, jnp.logical_and(eq, rank <= need))
        madd_scr[c] = jnp.where(sel, 0.0, NEG_MASK)
        return run + jnp.sum(e, axis=0, keepdims=True)

    lax.fori_loop(0, nch, select_chunk, jnp.zeros((1, qb), F32))

    m_scr[...] = jnp.full(m_scr.shape, -jnp.inf, F32)
    den_scr[...] = jnp.zeros(den_scr.shape, F32)
    acc_scr[...] = jnp.zeros(acc_scr.shape, F32)
    qa_all = qa_ref[...].reshape(ATT_HEADS * qb, KV_LORA)
    halves = qb // LANES

    def att_chunk(c, with_bias):
        off = pl.multiple_of(c * cw, cw)
        ckv_c = ckv_ref[pl.ds(off, cw), :]
        boff = pl.multiple_of(jnp.where(c == i, qb, 0), qb)
        l_scr[...] = _bdot_nt(ckv_c, qa_all)
        for g in range(ATT_HEADS * halves):
            cols = slice(g * LANES, (g + 1) * LANES)
            qcols = slice((g % halves) * LANES, (g % halves + 1) * LANES)
            l = l_scr[:, cols] + madd_scr[c, :, qcols]
            if with_bias:
                l = l + near_ref[g // halves, pl.ds(boff, qb), qcols]
            m_old = m_scr[:, cols]
            m_new = jnp.maximum(m_old, jnp.max(l, axis=0, keepdims=True))
            p = jnp.exp(l - m_new)
            scale = jnp.exp(m_old - m_new)
            den_scr[:, cols] = den_scr[:, cols] * scale + jnp.sum(p, axis=0, keepdims=True)
            m_scr[:, cols] = m_new
            sc_scr[:, cols] = scale
            p_scr[:, cols] = p.astype(BF16)
        pv = jnp.dot(ckvt_ref[c], p_scr[...], preferred_element_type=F32)
        acc_scr[...] = acc_scr[...] * sc_scr[...] + pv

    def far_body(c, carry):
        att_chunk(c, False)
        return carry

    def near_body(c, carry):
        att_chunk(c, True)
        return carry

    c_near = jnp.maximum(i - 1, 0)
    lax.fori_loop(0, c_near, far_body, 0)
    lax.fori_loop(c_near, nch, near_body, 0)

    o_rows = []
    for h in range(ATT_HEADS):
        o_lat_t = acc_scr[:, h * qb:(h + 1) * qb] / den_scr[:, h * qb:(h + 1) * qb]
        o_rows.append(_bdot(wuvt_ref[h], o_lat_t))
    h_out = _bdot(jnp.concatenate(o_rows, axis=0).T, wo_ref[...])
    o_ref[...] = _layer_norm(alpha * x_ref[...] + h_out, lng_ref[...], lnb_ref[...])


def _dsa_attn(x2d, batch, seq, qa, qi, ki, wit, ckv, ckvt, near, w_uv, w_o, ln_g, ln_b, alpha):
    t, d = x2d.shape
    qb = DSA_ROWS
    assert KV_LORA == LANES
    nb = seq // qb
    topk = min(TOPK_MAX, seq // 4)
    tril = (jnp.arange(qb)[:, None] >= jnp.arange(qb)[None, :]).astype(BF16)
    wuv_t = w_uv.transpose(0, 2, 1).astype(BF16)
    wo = w_o.astype(BF16)
    tok = lambda b, i: (b * nb + i, 0)
    in_specs = [
        pl.BlockSpec((IDX_HEADS, qb, IDX_DIM), lambda b, i: (0, b * nb + i, 0)),
        pl.BlockSpec((IDX_HEADS, qb), lambda b, i: (0, b * nb + i)),
        pl.BlockSpec((seq, IDX_DIM), lambda b, i: (b, 0)),
        pl.BlockSpec((seq, KV_LORA), lambda b, i: (b, 0)),
        pl.BlockSpec((nb, KV_LORA, qb), lambda b, i: (b, 0, 0)),
        pl.BlockSpec((ATT_HEADS, qb, KV_LORA), lambda b, i: (0, b * nb + i, 0)),
        pl.BlockSpec(near.shape, lambda b, i: (0, 0, 0), pipeline_mode=pl.Buffered(1)),
        _const_spec(tril.shape),
        _const_spec(wuv_t.shape),
        _const_spec(wo.shape),
        pl.BlockSpec((qb, d), tok),
        _const_spec((1, d)),
        _const_spec((1, d)),
    ]
    scratch = [pltpu.VMEM((nb, qb, qb), jnp.int32),
               pltpu.VMEM((nb, qb, qb), jnp.int16),
               pltpu.VMEM((nb, qb, qb), jnp.int16),
               pltpu.VMEM((nb, qb, qb), F32),
               pltpu.VMEM((1, ATT_HEADS * qb), F32),
               pltpu.VMEM((1, ATT_HEADS * qb), F32),
               pltpu.VMEM((KV_LORA, ATT_HEADS * qb), F32),
               pltpu.VMEM((1, ATT_HEADS * qb), F32),
               pltpu.VMEM((qb, ATT_HEADS * qb), F32),
               pltpu.VMEM((qb, ATT_HEADS * qb), BF16)]
    return pl.pallas_call(
        functools.partial(_dsa_attn_kernel, topk=topk, qb=qb, alpha=alpha),
        grid=(batch, nb),
        in_specs=in_specs,
        out_specs=pl.BlockSpec((qb, d), tok),
        out_shape=jax.ShapeDtypeStruct((t, d), F32),
        scratch_shapes=scratch,
        compiler_params=pltpu.CompilerParams(dimension_semantics=("parallel", "parallel"),
                                             vmem_limit_bytes=VMEM_LIMIT),
        name="dsa_attn",
    )(qi, wit, ki, ckv, ckvt, qa, near, tril, wuv_t, wo, x2d, ln_g.reshape(1, d), ln_b.reshape(1, d))


def kernel(x, ln_g, ln_b, rwkv_mix, rwkv_w_rkv, rwkv_w0, rwkv_w1, rwkv_w2, rwkv_a0, rwkv_a1, rwkv_a2, rwkv_v0, rwkv_v1, rwkv_v2, rwkv_g1, rwkv_g2, rwkv_k_k, rwkv_k_a, rwkv_r_k, rwkv_lnx_g, rwkv_lnx_b, rwkv_w_o, dsa_w_in, dsa_q_norm_g, dsa_kv_norm_g, dsa_w_uq, dsa_w_uk, dsa_w_uv, dsa_w_qidx, dsa_kidx_g, dsa_kidx_b, dsa_w_o, rel_bias, ffn_w_up, ffn_conv_w, ffn_conv_b, ffn_w_down):
    batch, seq, d = x.shape
    depth = ln_g.shape[0]
    heads = d // N_HEAD
    alpha = (2 * depth) ** 0.25
    gb = LANES // heads
    groups = batch // gb
    t = gb * seq
    xs = [x[g * gb:(g + 1) * gb].reshape(t, d) for g in range(groups)]

    def scan_param(v):
        return jnp.tile(v.reshape(heads, N_HEAD).T, (1, gb))

    near = _bias_tiles(rel_bias, DSA_ROWS)

    to_scan = functools.partial(_to_scan, gb=gb, seq=seq)

    v_first = [None] * groups
    for i in range(depth):
        j = i // 2
        for g in range(groups):
            x2d = xs[g]
            if i % 2 == 0:
                vres = None if j == 0 else (v_first[g], rwkv_v0[j - 1], rwkv_v1[j - 1], rwkv_v2[j - 1])
                r, w, k, v, lr, gate = _rwkv_proj(
                    x2d, seq, rwkv_mix[j], rwkv_w_rkv[j], rwkv_w0[j], rwkv_w1[j], rwkv_w2[j], rwkv_a0[j],
                    rwkv_a1[j], rwkv_a2[j], rwkv_g1[j], rwkv_g2[j], vres)
                if j == 0:
                    v_first[g] = v
                params = [scan_param(q) for q in (rwkv_k_k[j], rwkv_k_a[j], rwkv_r_k[j].reshape(-1),
                                                  rwkv_lnx_g[j], rwkv_lnx_b[j])]
                z = _wkv(to_scan(r), to_scan(w), to_scan(k), to_scan(v), to_scan(lr), params)
                x2d = _rwkv_out(_from_scan(z, gb, d), gate, x2d, rwkv_w_o[j], ln_g[i, 0], ln_b[i, 0], alpha)
            else:
                qa, qi, ki, wit, ckv, ckvt = _dsa_proj(x2d, dsa_w_in[j], dsa_q_norm_g[j], dsa_kv_norm_g[j],
                                                       dsa_w_uq[j], dsa_w_uk[j], dsa_w_qidx[j], dsa_kidx_g[j],
                                                       dsa_kidx_b[j])
                x2d = _dsa_attn(x2d, gb, seq, qa, qi, ki, wit, ckv, ckvt, near, dsa_w_uv[j], dsa_w_o[j],
                                ln_g[i, 0], ln_b[i, 0], alpha)
            xs[g] = _ffn(x2d, seq, ffn_w_up[i], ffn_conv_w[i], ffn_conv_b[i], ffn_w_down[i],
                         ln_g[i, 1], ln_b[i, 1], alpha)
    return jnp.concatenate([z.reshape(gb, seq, d) for z in xs], axis=0)
```

```python
import functools
import math

import jax
import jax.numpy as jnp
from jax import lax
from jax.experimental import pallas as pl
from jax.experimental.pallas import tpu as pltpu

F32 = jnp.float32
BF16 = jnp.bfloat16

N_HEAD = 64
ATT_HEADS = 16
QK_HEAD = 64
V_HEAD = 64
Q_LORA = 256
KV_LORA = 128
IDX_HEADS = 8
IDX_DIM = 64
TOPK_MAX = 256
DSA_ROWS = 256
REL_BUCKETS = 32
REL_MAX_DIST = 128
CONV_W = 3
GN_EPS = 64e-5
LN_EPS = 1e-5

LANES = 128
SUBLANES = 8
VMEM_LIMIT = 56 * 1024 * 1024
PAIR_PITCH = N_HEAD + 8
TIME_PITCH = LANES + 8
WKV_STEPS = 32
WKV_UNROLL = 8

NEG_MASK = -1e30
LOG2E = math.log2(math.e)
INT_MIN = -2 ** 31


def _bdot(a, b):
    return jnp.dot(a.astype(BF16), b.astype(BF16), preferred_element_type=F32)


def _bdot_nt(a, b):
    return lax.dot_general(a.astype(BF16), b.astype(BF16), (((1,), (1,)), ((), ())),
                           preferred_element_type=F32)


def _layer_norm(x, g, b):
    mu = jnp.mean(x, axis=-1, keepdims=True)
    xc = x - mu
    var = jnp.mean(xc * xc, axis=-1, keepdims=True)
    return xc * lax.rsqrt(var + LN_EPS) * g + b


def _const_spec(shape):
    nd = len(shape)
    return pl.BlockSpec(shape, lambda *_: (0,) * nd)


def _prev_rows_spec(ts, d):
    return pl.BlockSpec((SUBLANES, d), lambda i: (jnp.maximum(i * (ts // SUBLANES) - 1, 0), 0))


def _rwkv_proj_kernel(*refs, ts, seq, has_vres):
    (xp_ref, x_ref, mix_ref, wr_ref, wk_ref, wv_ref, w0_ref, w1_ref, w2_ref, a0_ref, a1_ref,
     a2_ref, g1_ref, g2_ref) = refs[:14]
    if has_vres:
        vf_ref, v0_ref, v1_ref, v2_ref = refs[14:18]
        outs = refs[18:]
    else:
        outs = refs[14:]
    r_o, w_o, k_o, v_o, a_o, g_o = outs

    first = (pl.program_id(0) * ts) % seq == 0
    x = x_ref[...]
    p_last = jnp.where(first, 0.0, xp_ref[SUBLANES - 1:SUBLANES, :])
    rows = lax.broadcasted_iota(jnp.int32, (ts, 1), 0)
    x_shift = jnp.where(rows == 0, p_last, pltpu.roll(x, 1, 0))
    xx = x_shift - x
    mix = mix_ref[...]

    def xm(j):
        return x + xx * mix[j:j + 1]

    r = _bdot(xm(0), wr_ref[...])
    k = _bdot(xm(1), wk_ref[...])
    xv = xm(2)
    v = _bdot(xv, wv_ref[...])
    w_log = -jax.nn.softplus(-(w0_ref[...] + _bdot(jnp.tanh(_bdot(xm(3), w1_ref[...])), w2_ref[...]))) - 0.5
    decay = jnp.exp(-jnp.exp(w_log))
    if has_vres:
        v = v + (vf_ref[...] - v) * jax.nn.sigmoid(v0_ref[...] + _bdot(_bdot(xv, v1_ref[...]), v2_ref[...]))
    a = jax.nn.sigmoid(a0_ref[...] + _bdot(_bdot(xm(4), a1_ref[...]), a2_ref[...]))
    g = _bdot(jax.nn.sigmoid(_bdot(xm(5), g1_ref[...])), g2_ref[...])
    r_o[...] = r
    w_o[...] = decay
    k_o[...] = k
    v_o[...] = v
    a_o[...] = a
    g_o[...] = g


def _rwkv_proj(x2d, seq, mix, w_rkv, w0, w1, w2, a0, a1, a2, g1, g2, vres, ts=512):
    t, d = x2d.shape
    row = lambda v: v.reshape(1, d)
    tok = pl.BlockSpec((ts, d), lambda i: (i, 0))
    ins = [x2d, x2d, mix, w_rkv[0].astype(BF16), w_rkv[1].astype(BF16), w_rkv[2].astype(BF16),
           row(w0), w1.astype(BF16), w2.astype(BF16), row(a0), a1.astype(BF16), a2.astype(BF16),
           g1.astype(BF16), g2.astype(BF16)]
    specs = [_prev_rows_spec(ts, d), tok] + [_const_spec(a.shape) for a in ins[2:]]
    if vres is not None:
        v_first, v0, v1, v2 = vres
        extra = [v_first, row(v0), v1.astype(BF16), v2.astype(BF16)]
        ins += extra
        specs += [tok] + [_const_spec(a.shape) for a in extra[1:]]
    out_shape = [jax.ShapeDtypeStruct((t, d), F32)] * 6
    return pl.pallas_call(
        functools.partial(_rwkv_proj_kernel, ts=ts, seq=seq, has_vres=vres is not None),
        grid=(t // ts,),
        in_specs=specs,
        out_specs=[tok] * 6,
        out_shape=out_shape,
        compiler_params=pltpu.CompilerParams(dimension_semantics=("parallel",),
                                             vmem_limit_bytes=VMEM_LIMIT),
        name="rwkv_proj",
    )(*ins)


def _to_scan_kernel(x_ref, o_ref, zt_scr, w_scr, *, gb, d, n):
    heads = d // n
    per_tile = LANES // n
    for b in range(gb):
        for j in range(d // LANES):
            xt = x_ref[b, :, j * LANES:(j + 1) * LANES].T
            for hh in range(per_tile):
                h = j * per_tile + hh
                zt_scr[pl.ds((b * heads + h) * PAIR_PITCH, n), :] = xt[hh * n:(hh + 1) * n]
    tt = o_ref.shape[0]
    for c in range(n):
        w_scr[pl.ds(c * TIME_PITCH, tt), :] = zt_scr[pl.ds(c, gb * heads, stride=PAIR_PITCH), :].T
    for t in range(tt):
        o_ref[t] = w_scr[pl.ds(t, n, stride=TIME_PITCH), :]


def _to_scan(x2d, gb, seq):
    t, d = x2d.shape
    tt = LANES
    n = N_HEAD
    pairs = gb * d // n
    return pl.pallas_call(
        functools.partial(_to_scan_kernel, gb=gb, d=d, n=n),
        grid=(seq // tt,),
        in_specs=[pl.BlockSpec((gb, tt, d), lambda i: (0, i, 0))],
        out_specs=pl.BlockSpec((tt, n, pairs), lambda i: (i, 0, 0)),
        out_shape=jax.ShapeDtypeStruct((seq, n, pairs), F32),
        scratch_shapes=[pltpu.VMEM((pairs * PAIR_PITCH, tt), F32), pltpu.VMEM((n * TIME_PITCH, pairs), F32)],
        compiler_params=pltpu.CompilerParams(dimension_semantics=("parallel",), vmem_limit_bytes=VMEM_LIMIT),
        name="to_scan_layout",
    )(x2d.reshape(gb, seq, d))


def _from_scan_kernel(y_ref, o_ref, zt_scr, w_scr, *, gb, d, n):
    heads = d // n
    tt = y_ref.shape[0]
    for t in range(tt):
        w_scr[pl.ds(t, n, stride=TIME_PITCH), :] = y_ref[t]
    for c in range(n):
        zt_scr[pl.ds(c, gb * heads, stride=PAIR_PITCH), :] = w_scr[pl.ds(c * TIME_PITCH, tt), :].T
    per_tile = LANES // n
    for b in range(gb):
        for j in range(d // LANES):
            xt = jnp.concatenate([zt_scr[pl.ds((b * heads + j * per_tile + hh) * PAIR_PITCH, n), :]
                                  for hh in range(per_tile)], axis=0)
            o_ref[b, :, j * LANES:(j + 1) * LANES] = xt.T


def _from_scan(y, gb, d):
    seq, n, pairs = y.shape
    tt = LANES
    return pl.pallas_call(
        functools.partial(_from_scan_kernel, gb=gb, d=d, n=n),
        grid=(seq // tt,),
        in_specs=[pl.BlockSpec((tt, n, pairs), lambda i: (i, 0, 0))],
        out_specs=pl.BlockSpec((gb, tt, d), lambda i: (0, i, 0)),
        out_shape=jax.ShapeDtypeStruct((gb, seq, d), F32),
        scratch_shapes=[pltpu.VMEM((pairs * PAIR_PITCH, tt), F32), pltpu.VMEM((n * TIME_PITCH, pairs), F32)],
        compiler_params=pltpu.CompilerParams(dimension_semantics=("parallel",), vmem_limit_bytes=VMEM_LIMIT),
        name="from_scan_layout",
    )(y).reshape(gb * seq, d)


def _rows_of_sums(parts):
    row = lax.broadcasted_iota(jnp.int32, parts[0].shape, 0)
    dist = SUBLANES // 2
    while dist >= 1:
        lower = (row & dist) == 0
        half = len(parts) // 2
        parts = [jnp.where(lower, parts[j] + pltpu.roll(parts[j], SUBLANES - dist, 0),
                           parts[j + half] + pltpu.roll(parts[j + half], dist, 0)) for j in range(half)]
        dist //= 2
    return parts[0]


def _wkv_kernel(r_ref, w_ref, k_ref, v_ref, al_ref, kkp_ref, kap_ref, rk_ref, lxg_ref, lxb_ref, y_ref, s_scr,
                *, tc, n):
    @pl.when(pl.program_id(1) == 0)
    def _():
        s_scr[...] = jnp.zeros_like(s_scr)

    groups = n // SUBLANES
    inv_n = 1.0 / n

    def step(t, c_prev):
        k = k_ref[t]
        r = r_ref[t]
        lr = al_ref[t]
        kk = k * kkp_ref[...]
        kk = kk / jnp.maximum(jnp.sqrt(jnp.sum(kk * kk, axis=0, keepdims=True)), 1e-12)
        k = k * (1.0 + (lr - 1.0) * kap_ref[...])
        c_now = c_prev * w_ref[t]
        inv_c = 1.0 / c_now
        a = -kk * c_prev
        b = kk * lr
        b_s = b * inv_c
        k_s = k * inv_c
        wr = c_now * r
        br = jnp.sum(b * r, axis=0, keepdims=True)
        kr = jnp.sum(k * r, axis=0, keepdims=True)

        def vblock(vb, c):
            base = pl.multiple_of(vb * SUBLANES, SUBLANES)
            pa, py = [], []
            for j in range(SUBLANES):
                s = s_scr[base + j]
                pa.append(jnp.sum((s * a).reshape(groups, SUBLANES, LANES), axis=0))
                py.append(jnp.sum((s * wr).reshape(groups, SUBLANES, LANES), axis=0))
            sa = _rows_of_sums(pa)
            vv = v_ref[t, pl.ds(base, SUBLANES), :]
            y_ref[t, pl.ds(base, SUBLANES), :] = _rows_of_sums(py) + sa * br + vv * kr
            for j in range(SUBLANES):
                s_scr[base + j] = s_scr[base + j] + sa[j:j + 1] * b_s + vv[j:j + 1] * k_s
            return c

        lax.fori_loop(0, groups, vblock, 0, unroll=WKV_UNROLL)

        y = y_ref[t]
        yc = y - jnp.sum(y, axis=0, keepdims=True) * inv_n
        var = jnp.sum(yc * yc, axis=0, keepdims=True) * inv_n
        bonus = jnp.sum(r * k * rk_ref[...], axis=0, keepdims=True) * v_ref[t]
        y_ref[t] = yc * lax.rsqrt(var + GN_EPS) * lxg_ref[...] + lxb_ref[...] + bonus
        return c_now

    c_end = lax.fori_loop(0, tc, step, jnp.ones((n, LANES), F32))
    s_scr[...] = s_scr[...] * c_end[None]


def _wkv(r, w, k, v, lr, params, tc=WKV_STEPS):
    s, n, pairs = r.shape
    spec = pl.BlockSpec((tc, n, LANES), lambda p, c: (c, 0, p))
    pspec = pl.BlockSpec((n, LANES), lambda p, c: (0, p))
    return pl.pallas_call(
        functools.partial(_wkv_kernel, tc=tc, n=n),
        grid=(pairs // LANES, s // tc),
        in_specs=[spec] * 5 + [pspec] * len(params),
        out_specs=spec,
        out_shape=jax.ShapeDtypeStruct((s, n, pairs), F32),
        scratch_shapes=[pltpu.VMEM((n, n, LANES), F32)],
        compiler_params=pltpu.CompilerParams(dimension_semantics=("parallel", "arbitrary"),
                                             vmem_limit_bytes=VMEM_LIMIT),
        name="wkv7_scan",
    )(r, w, k, v, lr, *params)


def _rwkv_out_kernel(z_ref, g_ref, x_ref, wo_ref, lng_ref, lnb_ref, o_ref, *, alpha):
    h = _bdot(z_ref[...] * g_ref[...], wo_ref[...])
    o_ref[...] = _layer_norm(alpha * x_ref[...] + h, lng_ref[...], lnb_ref[...])


def _rwkv_out(z, g, x2d, w_o, ln_g, ln_b, alpha, ts=512):
    t, d = x2d.shape
    row = lambda a: a.reshape(1, d)
    tok = pl.BlockSpec((ts, d), lambda i: (i, 0))
    consts = [w_o.astype(BF16), row(ln_g), row(ln_b)]
    return pl.pallas_call(
        functools.partial(_rwkv_out_kernel, alpha=alpha),
        grid=(t // ts,),
        in_specs=[tok] * 3 + [_const_spec(a.shape) for a in consts],
        out_specs=tok,
        out_shape=jax.ShapeDtypeStruct((t, d), F32),
        compiler_params=pltpu.CompilerParams(dimension_semantics=("parallel",),
                                             vmem_limit_bytes=VMEM_LIMIT),
        name="rwkv_out",
    )(z, g, x2d, *consts)


def _ffn_kernel(xp_ref, x_ref, wup_ref, cw_ref, cb_ref, wdn_ref, lng_ref, lnb_ref, o_ref, u_scr,
                *, ts, seq, d_ff, fc, alpha):
    first = (pl.program_id(0) * ts) % seq == 0
    x = x_ref[...]
    xp = jnp.where(first, 0.0, xp_ref[...])
    xe = jnp.concatenate([xp, x], axis=0).astype(BF16)

    def conv_cols(off):
        u_scr[...] = jnp.dot(xe, wup_ref[:, off:off + fc], preferred_element_type=F32)
        cw = cw_ref[:, off:off + fc]
        return (u_scr[SUBLANES - 2:SUBLANES - 2 + ts, :] * cw[0:1]
                + u_scr[SUBLANES - 1:SUBLANES - 1 + ts, :] * cw[1:2]
                + u_scr[SUBLANES:SUBLANES + ts, :] * cw[2:3]
                + cb_ref[:, off:off + fc])

    acc = jnp.zeros(x.shape, F32)
    for c in range(d_ff // fc):
        gate = conv_cols(c * fc)
        val = conv_cols(d_ff + c * fc)
        act = jax.nn.silu(gate) * val
        acc = acc + jnp.dot(act.astype(BF16), wdn_ref[c * fc:(c + 1) * fc, :], preferred_element_type=F32)
    o_ref[...] = _layer_norm(alpha * x + acc, lng_ref[...], lnb_ref[...])


def _ffn(x2d, seq, w_up, conv_w, conv_b, w_down, ln_g, ln_b, alpha, ts=512):
    t, d = x2d.shape
    d_ff = w_down.shape[0]
    fc = d_ff
    tok = pl.BlockSpec((ts, d), lambda i: (i, 0))
    single = pl.Buffered(1)
    wup_spec = pl.BlockSpec((d, 2 * d_ff), lambda i: (0, 0), pipeline_mode=single)
    wdn_spec = pl.BlockSpec((d_ff, d), lambda i: (0, 0), pipeline_mode=single)
    consts = [conv_w, conv_b.reshape(1, -1)]
    rows = [ln_g.reshape(1, d), ln_b.reshape(1, d)]
    return pl.pallas_call(
        functools.partial(_ffn_kernel, ts=ts, seq=seq, d_ff=d_ff, fc=fc, alpha=alpha),
        grid=(t // ts,),
        in_specs=[_prev_rows_spec(ts, d), tok, wup_spec] + [_const_spec(a.shape) for a in consts]
                 + [wdn_spec] + [_const_spec(a.shape) for a in rows],
        out_specs=tok,
        out_shape=jax.ShapeDtypeStruct((t, d), F32),
        scratch_shapes=[pltpu.VMEM((ts + SUBLANES, fc), F32)],
        compiler_params=pltpu.CompilerParams(dimension_semantics=("parallel",),
                                             vmem_limit_bytes=VMEM_LIMIT),
        name="conv_ffn",
    )(x2d, x2d, w_up.astype(BF16), *consts, w_down.astype(BF16), *rows)


def _dsa_proj_kernel(x_ref, wcq_ref, wckv_ref, wki_ref, wwit_ref, qg_ref, kvg_ref, wuq_ref, wukp_ref,
                     wqi_ref, kig_ref, kib_ref, qa_o, qi_o, ki_o, wit_o, ckv_o, ckvt_o, *, cw):
    x = x_ref[...].astype(BF16)
    c_q = jnp.dot(x, wcq_ref[...], preferred_element_type=F32)
    c_kv = jnp.dot(x, wckv_ref[...], preferred_element_type=F32)
    k_idx = jnp.dot(x, wki_ref[...], preferred_element_type=F32)
    w_idx_t = _bdot_nt(wwit_ref[...], x)

    def rms(z, g):
        return z * lax.rsqrt(jnp.mean(z * z, axis=-1, keepdims=True) + 1e-6) * g

    c_q = rms(c_q, qg_ref[...]).astype(BF16)
    c_kv = rms(c_kv, kvg_ref[...])
    ckv_o[...] = c_kv.astype(BF16)
    for jc in range(c_kv.shape[0] // cw):
        ckvt_o[jc] = c_kv[jc * cw:(jc + 1) * cw].T.astype(BF16)
    q = jnp.dot(c_q, wuq_ref[...], preferred_element_type=F32)
    qk_scale = QK_HEAD ** -0.5 * LOG2E
    for hp in range(ATT_HEADS // 2):
        qa = _bdot(q[:, hp * LANES:(hp + 1) * LANES], wukp_ref[hp]) * qk_scale
        qa_o[2 * hp] = qa[:, :KV_LORA].astype(BF16)
        qa_o[2 * hp + 1] = qa[:, KV_LORA:].astype(BF16)
    for h in range(IDX_HEADS):
        qi_o[h] = jnp.dot(c_q, wqi_ref[h], preferred_element_type=F32).astype(BF16)
    ki_o[...] = _layer_norm(k_idx, kig_ref[...], kib_ref[...]).astype(BF16)
    wit_o[...] = w_idx_t * (IDX_HEADS ** -0.5 * IDX_DIM ** -0.5)


def _dsa_proj(x2d, w_in, q_norm_g, kv_norm_g, w_uq, w_uk, w_qidx, kidx_g, kidx_b, ts=512):
    t, d = x2d.shape
    cw = DSA_ROWS
    o1, o2, o3 = Q_LORA, Q_LORA + KV_LORA, Q_LORA + KV_LORA + IDX_DIM
    w_in = w_in.astype(BF16)
    z = jnp.zeros((ATT_HEADS // 2, QK_HEAD, KV_LORA), F32)
    wuk_pair = jnp.concatenate([jnp.concatenate([w_uk[0::2], z], axis=2),
                                jnp.concatenate([z, w_uk[1::2]], axis=2)], axis=1).astype(BF16)
    wqi = w_qidx.reshape(Q_LORA, IDX_HEADS, IDX_DIM).transpose(1, 0, 2).astype(BF16)
    consts = [w_in[:, :o1], w_in[:, o1:o2], w_in[:, o2:o3], w_in[:, o3:].T, q_norm_g.reshape(1, -1),
              kv_norm_g.reshape(1, -1), w_uq.astype(BF16), wuk_pair, wqi, kidx_g.reshape(1, -1),
              kidx_b.reshape(1, -1)]
    out_shape = [jax.ShapeDtypeStruct((ATT_HEADS, t, KV_LORA), BF16),
                 jax.ShapeDtypeStruct((IDX_HEADS, t, IDX_DIM), BF16),
                 jax.ShapeDtypeStruct((t, IDX_DIM), BF16),
                 jax.ShapeDtypeStruct((IDX_HEADS, t), F32),
                 jax.ShapeDtypeStruct((t, KV_LORA), BF16),
                 jax.ShapeDtypeStruct((t // cw, KV_LORA, cw), BF16)]
    out_specs = [pl.BlockSpec((ATT_HEADS, ts, KV_LORA), lambda i: (0, i, 0)),
                 pl.BlockSpec((IDX_HEADS, ts, IDX_DIM), lambda i: (0, i, 0)),
                 pl.BlockSpec((ts, IDX_DIM), lambda i: (i, 0)),
                 pl.BlockSpec((IDX_HEADS, ts), lambda i: (0, i)),
                 pl.BlockSpec((ts, KV_LORA), lambda i: (i, 0)),
                 pl.BlockSpec((ts // cw, KV_LORA, cw), lambda i: (i, 0, 0))]
    return pl.pallas_call(
        functools.partial(_dsa_proj_kernel, cw=cw),
        grid=(t // ts,),
        in_specs=[pl.BlockSpec((ts, d), lambda i: (i, 0))] + [_const_spec(a.shape) for a in consts],
        out_specs=out_specs,
        out_shape=out_shape,
        compiler_params=pltpu.CompilerParams(dimension_semantics=("parallel",),
                                             vmem_limit_bytes=VMEM_LIMIT),
        name="dsa_proj",
    )(x2d, *consts)


def _bias_tiles_kernel(rb_ref, o_ref, *, qb):
    c = lax.broadcasted_iota(jnp.int32, (2 * qb, qb), 0)
    tl = lax.broadcasted_iota(jnp.int32, (2 * qb, qb), 1)
    n = jnp.maximum(qb + tl - c, 0)
    max_exact = REL_BUCKETS // 2
    nf = jnp.maximum(n, 1).astype(F32)
    large = max_exact + (jnp.log(nf / max_exact) / math.log(REL_MAX_DIST / max_exact)
                         * (REL_BUCKETS - max_exact)).astype(jnp.int32)
    large = jnp.minimum(large, REL_BUCKETS - 1)
    bucket = jnp.where(n < max_exact, n, large)
    for h in range(ATT_HEADS):
        acc = jnp.zeros((2 * qb, qb), F32)
        for bkt in range(REL_BUCKETS):
            acc = jnp.where(bucket == bkt, rb_ref[bkt, h], acc)
        o_ref[h] = (acc - rb_ref[REL_BUCKETS - 1, h]) * LOG2E


def _bias_tiles(rel_bias, qb):
    return pl.pallas_call(
        functools.partial(_bias_tiles_kernel, qb=qb),
        in_specs=[pl.BlockSpec(memory_space=pltpu.SMEM)],
        out_specs=pl.BlockSpec(memory_space=pltpu.VMEM),
        out_shape=jax.ShapeDtypeStruct((ATT_HEADS, 2 * qb, qb), F32),
        name="rel_bias_tiles",
    )(rel_bias)


def _dsa_attn_kernel(qi_ref, wit_ref, ki_ref, ckv_ref, ckvt_ref, qa_ref, near_ref, tril_ref, wuvt_ref,
                     wo_ref, x_ref, lng_ref, lnb_ref, o_ref, key_scr, khi_scr, klo_scr, madd_scr, m_scr, den_scr, acc_scr,
                     sc_scr, p_scr,
                     *, topk, qb, alpha):
    cw = qb
    i = pl.program_id(1)
    nch = i + 1
    t_pos = i * qb + lax.broadcasted_iota(jnp.int32, (1, qb), 1)
    s_loc = lax.broadcasted_iota(jnp.int32, (cw, 1), 0)
    int_min = jnp.int32(INT_MIN)

    qi_all = qi_ref[...].reshape(IDX_HEADS * qb, IDX_DIM)
    wit = wit_ref[...]

    def score_chunk(c, carry):
        off = pl.multiple_of(c * cw, cw)
        s_all = _bdot_nt(ki_ref[pl.ds(off, cw), :], qi_all)
        score = jnp.zeros((cw, qb), F32)
        for h in range(IDX_HEADS):
            score = score + jnp.maximum(s_all[:, h * qb:(h + 1) * qb], 0.0) * wit[h:h + 1]
        bits = lax.bitcast_convert_type(score, jnp.int32)
        key = jnp.where(bits < 0, bits ^ jnp.int32(0x7FFFFFFF), bits)
        key = jnp.where(off + s_loc <= t_pos, key, int_min)
        key_scr[c] = key
        khi_scr[c] = lax.shift_right_arithmetic(key, 16).astype(jnp.int16)
        klo_scr[c] = ((key & 0xFFFF) - 32768).astype(jnp.int16)
        return carry

    lax.fori_loop(0, nch, score_chunk, 0)

    def count(pred):
        def body(c, acc):
            kc = key_scr[c]
            for r in range(cw // SUBLANES):
                acc = acc + jnp.where(pred(kc[r * SUBLANES:(r + 1) * SUBLANES]), 1.0, 0.0)
            return acc

        acc = lax.fori_loop(0, nch, body, jnp.zeros((SUBLANES, qb), F32))
        return jnp.sum(acc, axis=0, keepdims=True)

    rows16 = 2 * SUBLANES
    one16, zero16, min16 = jnp.int16(1), jnp.int16(0), jnp.int16(-32768)

    def count16(scr, pred):
        def body(c, acc):
            kc = scr[c]
            for r in range(cw // rows16):
                acc = acc + jnp.where(pred(kc[r * rows16:(r + 1) * rows16]), one16, zero16)
            return acc

        acc = lax.fori_loop(0, nch, body, jnp.zeros((rows16, qb), jnp.int16))
        return jnp.sum(acc.astype(jnp.int32), axis=0, keepdims=True)

    def search16(scr, want):
        tau = jnp.where(count16(scr, lambda x: x >= zero16) >= want, jnp.int32(0), jnp.int32(-32768))

        def bit_step(bi, tau):
            cand = tau | lax.shift_left(jnp.int32(1), jnp.int32(14) - bi)
            c16 = cand.astype(jnp.int16)
            return jnp.where(count16(scr, lambda x: x >= c16) >= want, cand, tau)

        return lax.fori_loop(0, 15, bit_step, tau)

    tau_hi = search16(khi_scr, jnp.int32(topk))
    th16 = tau_hi.astype(jnp.int16)
    want_lo = topk - count16(khi_scr, lambda x: x > th16)

    def mask_chunk(c, carry):
        klo_scr[c] = jnp.where(khi_scr[c] == th16, klo_scr[c], min16)
        return carry

    lax.fori_loop(0, nch, mask_chunk, 0)
    tau_lo = search16(klo_scr, want_lo)
    tau = lax.shift_left(tau_hi, 16) | ((tau_lo + 32768) & 0xFFFF)
    kf = float(topk)

    need = kf - count(lambda kc: kc > tau)
    has_kth = tau > int_min
    tril = tril_ref[...]

    def select_chunk(c, run):
        kc = key_scr[c]
        eq = jnp.logical_and(kc == tau, has_kth)
        e = jnp.where(eq, 1.0, 0.0)
        rank = jnp.dot(tril, e.astype(BF16), preferred_element_type=F32) + run
        sel = jnp.logical_or(kc > tau, jnp.logical_and(eq, rank <= need))
        madd_scr[c] = jnp.where(sel, 0.0, NEG_MASK)
        return run + jnp.sum(e, axis=0, keepdims=True)

    lax.fori_loop(0, nch, select_chunk, jnp.zeros((1, qb), F32))

    m_scr[...] = jnp.full(m_scr.shape, -jnp.inf, F32)
    den_scr[...] = jnp.zeros(den_scr.shape, F32)
    acc_scr[...] = jnp.zeros(acc_scr.shape, F32)
    qa_all = qa_ref[...].reshape(ATT_HEADS * qb, KV_LORA)
    halves = qb // LANES

    def att_chunk(c, with_bias):
        off = pl.multiple_of(c * cw, cw)
        ckv_c = ckv_ref[pl.ds(off, cw), :]
        boff = pl.multiple_of(jnp.where(c == i, qb, 0), qb)
        logits = _bdot_nt(ckv_c, qa_all)
        for g in range(ATT_HEADS * halves):
            cols = slice(g * LANES, (g + 1) * LANES)
            qcols = slice((g % halves) * LANES, (g % halves + 1) * LANES)
            l = logits[:, cols] + madd_scr[c, :, qcols]
            if with_bias:
                l = l + near_ref[g // halves, pl.ds(boff, qb), qcols]
            m_old = m_scr[:, cols]
            m_new = jnp.maximum(m_old, jnp.max(l, axis=0, keepdims=True))
            p = jnp.exp2(l - m_new)
            scale = jnp.exp2(m_old - m_new)
            den_scr[:, cols] = den_scr[:, cols] * scale + jnp.sum(p, axis=0, keepdims=True)
            m_scr[:, cols] = m_new
            sc_scr[:, cols] = scale
            p_scr[:, cols] = p.astype(BF16)
        pv = jnp.dot(ckvt_ref[c], p_scr[...], preferred_element_type=F32)
        acc_scr[...] = acc_scr[...] * sc_scr[...] + pv

    def far_body(c, carry):
        att_chunk(c, False)
        return carry

    def near_body(c, carry):
        att_chunk(c, True)
        return carry

    c_near = jnp.maximum(i - 1, 0)
    lax.fori_loop(0, c_near, far_body, 0)
    lax.fori_loop(c_near, nch, near_body, 0)

    o_rows = []
    for h in range(ATT_HEADS):
        o_lat_t = acc_scr[:, h * qb:(h + 1) * qb] / den_scr[:, h * qb:(h + 1) * qb]
        o_rows.append(_bdot(wuvt_ref[h], o_lat_t))
    h_out = _bdot(jnp.concatenate(o_rows, axis=0).T, wo_ref[...])
    o_ref[...] = _layer_norm(alpha * x_ref[...] + h_out, lng_ref[...], lnb_ref[...])


def _dsa_attn(x2d, batch, seq, qa, qi, ki, wit, ckv, ckvt, near, w_uv, w_o, ln_g, ln_b, alpha):
    t, d = x2d.shape
    qb = DSA_ROWS
    assert KV_LORA == LANES
    nb = seq // qb
    topk = min(TOPK_MAX, seq // 4)
    tril = (jnp.arange(qb)[:, None] >= jnp.arange(qb)[None, :]).astype(BF16)
    wuv_t = w_uv.transpose(0, 2, 1).astype(BF16)
    wo = w_o.astype(BF16)
    tok = lambda b, i: (b * nb + i, 0)
    in_specs = [
        pl.BlockSpec((IDX_HEADS, qb, IDX_DIM), lambda b, i: (0, b * nb + i, 0)),
        pl.BlockSpec((IDX_HEADS, qb), lambda b, i: (0, b * nb + i)),
        pl.BlockSpec((seq, IDX_DIM), lambda b, i: (b, 0)),
        pl.BlockSpec((seq, KV_LORA), lambda b, i: (b, 0)),
        pl.BlockSpec((nb, KV_LORA, qb), lambda b, i: (b, 0, 0)),
        pl.BlockSpec((ATT_HEADS, qb, KV_LORA), lambda b, i: (0, b * nb + i, 0)),
        pl.BlockSpec(near.shape, lambda b, i: (0, 0, 0), pipeline_mode=pl.Buffered(1)),
        _const_spec(tril.shape),
        _const_spec(wuv_t.shape),
        _const_spec(wo.shape),
        pl.BlockSpec((qb, d), tok),
        _const_spec((1, d)),
        _const_spec((1, d)),
    ]
    scratch = [pltpu.VMEM((nb, qb, qb), jnp.int32),
               pltpu.VMEM((nb, qb, qb), jnp.int16),
               pltpu.VMEM((nb, qb, qb), jnp.int16),
               pltpu.VMEM((nb, qb, qb), F32),
               pltpu.VMEM((1, ATT_HEADS * qb), F32),
               pltpu.VMEM((1, ATT_HEADS * qb), F32),
               pltpu.VMEM((KV_LORA, ATT_HEADS * qb), F32),
               pltpu.VMEM((1, ATT_HEADS * qb), F32),
               pltpu.VMEM((qb, ATT_HEADS * qb), BF16)]
    return pl.pallas_call(
        functools.partial(_dsa_attn_kernel, topk=topk, qb=qb, alpha=alpha),
        grid=(batch, nb),
        in_specs=in_specs,
        out_specs=pl.BlockSpec((qb, d), tok),
        out_shape=jax.ShapeDtypeStruct((t, d), F32),
        scratch_shapes=scratch,
        compiler_params=pltpu.CompilerParams(dimension_semantics=("parallel", "parallel"),
                                             vmem_limit_bytes=VMEM_LIMIT),
        name="dsa_attn",
    )(qi, wit, ki, ckv, ckvt, qa, near, tril, wuv_t, wo, x2d, ln_g.reshape(1, d), ln_b.reshape(1, d))


def kernel(x, ln_g, ln_b, rwkv_mix, rwkv_w_rkv, rwkv_w0, rwkv_w1, rwkv_w2, rwkv_a0, rwkv_a1, rwkv_a2, rwkv_v0, rwkv_v1, rwkv_v2, rwkv_g1, rwkv_g2, rwkv_k_k, rwkv_k_a, rwkv_r_k, rwkv_lnx_g, rwkv_lnx_b, rwkv_w_o, dsa_w_in, dsa_q_norm_g, dsa_kv_norm_g, dsa_w_uq, dsa_w_uk, dsa_w_uv, dsa_w_qidx, dsa_kidx_g, dsa_kidx_b, dsa_w_o, rel_bias, ffn_w_up, ffn_conv_w, ffn_conv_b, ffn_w_down):
    batch, seq, d = x.shape
    depth = ln_g.shape[0]
    heads = d // N_HEAD
    alpha = (2 * depth) ** 0.25
    gb = LANES // heads
    groups = batch // gb
    t = gb * seq
    xs = [x[g * gb:(g + 1) * gb].reshape(t, d) for g in range(groups)]

    def scan_param(v):
        return jnp.tile(v.reshape(heads, N_HEAD).T, (1, gb))

    near = _bias_tiles(rel_bias, DSA_ROWS)

    to_scan = functools.partial(_to_scan, gb=gb, seq=seq)

    v_first = [None] * groups
    for i in range(depth):
        j = i // 2
        for g in range(groups):
            x2d = xs[g]
            if i % 2 == 0:
                vres = None if j == 0 else (v_first[g], rwkv_v0[j - 1], rwkv_v1[j - 1], rwkv_v2[j - 1])
                r, w, k, v, lr, gate = _rwkv_proj(
                    x2d, seq, rwkv_mix[j], rwkv_w_rkv[j], rwkv_w0[j], rwkv_w1[j], rwkv_w2[j], rwkv_a0[j],
                    rwkv_a1[j], rwkv_a2[j], rwkv_g1[j], rwkv_g2[j], vres)
                if j == 0:
                    v_first[g] = v
                params = [scan_param(q) for q in (rwkv_k_k[j], rwkv_k_a[j], rwkv_r_k[j].reshape(-1),
                                                  rwkv_lnx_g[j], rwkv_lnx_b[j])]
                z = _wkv(to_scan(r), to_scan(w), to_scan(k), to_scan(v), to_scan(lr), params)
                x2d = _rwkv_out(_from_scan(z, gb, d), gate, x2d, rwkv_w_o[j], ln_g[i, 0], ln_b[i, 0], alpha)
            else:
                qa, qi, ki, wit, ckv, ckvt = _dsa_proj(x2d, dsa_w_in[j], dsa_q_norm_g[j], dsa_kv_norm_g[j],
                                                       dsa_w_uq[j], dsa_w_uk[j], dsa_w_qidx[j], dsa_kidx_g[j],
                                                       dsa_kidx_b[j])
                x2d = _dsa_attn(x2d, gb, seq, qa, qi, ki, wit, ckv, ckvt, near, dsa_w_uv[j], dsa_w_o[j],
                                ln_g[i, 0], ln_b[i, 0], alpha)
            xs[g] = _ffn(x2d, seq, ffn_w_up[i], ffn_conv_w[i], ffn_conv_b[i], ffn_w_down[i],
                         ln_g[i, 1], ln_b[i, 1], alpha)
    return jnp.concatenate([z.reshape(gb, seq, d) for z in xs], axis=0)
```

```python
import functools
import math

import jax
import jax.numpy as jnp
from jax import lax
from jax.experimental import pallas as pl
from jax.experimental.pallas import tpu as pltpu

F32 = jnp.float32
BF16 = jnp.bfloat16

N_HEAD = 64
ATT_HEADS = 16
QK_HEAD = 64
V_HEAD = 64
Q_LORA = 256
KV_LORA = 128
IDX_HEADS = 8
IDX_DIM = 64
TOPK_MAX = 256
DSA_ROWS = 256
REL_BUCKETS = 32
REL_MAX_DIST = 128
CONV_W = 3
GN_EPS = 64e-5
LN_EPS = 1e-5

LANES = 128
SUBLANES = 8
VMEM_LIMIT = 56 * 1024 * 1024
PAIR_PITCH = N_HEAD + 8
TIME_PITCH = LANES + 8
WKV_STEPS = 32
WKV_UNROLL = 8

NEG_MASK = -1e30
LOG2E = math.log2(math.e)
INT_MIN = -2 ** 31


def _bdot(a, b):
    return jnp.dot(a.astype(BF16), b.astype(BF16), preferred_element_type=F32)


def _bdot_nt(a, b):
    return lax.dot_general(a.astype(BF16), b.astype(BF16), (((1,), (1,)), ((), ())),
                           preferred_element_type=F32)


def _layer_norm(x, g, b):
    mu = jnp.mean(x, axis=-1, keepdims=True)
    xc = x - mu
    var = jnp.mean(xc * xc, axis=-1, keepdims=True)
    return xc * lax.rsqrt(var + LN_EPS) * g + b


def _const_spec(shape):
    nd = len(shape)
    return pl.BlockSpec(shape, lambda *_: (0,) * nd)


def _prev_rows_spec(ts, d):
    return pl.BlockSpec((SUBLANES, d), lambda i: (jnp.maximum(i * (ts // SUBLANES) - 1, 0), 0))


def _rwkv_proj_kernel(*refs, ts, seq, has_vres):
    (xp_ref, x_ref, mix_ref, wr_ref, wk_ref, wv_ref, w0_ref, w1_ref, w2_ref, a0_ref, a1_ref,
     a2_ref, g1_ref, g2_ref) = refs[:14]
    if has_vres:
        vf_ref, v0_ref, v1_ref, v2_ref = refs[14:18]
        outs = refs[18:]
    else:
        outs = refs[14:]
    r_o, w_o, k_o, v_o, a_o, g_o = outs

    first = (pl.program_id(0) * ts) % seq == 0
    x = x_ref[...]
    p_last = jnp.where(first, 0.0, xp_ref[SUBLANES - 1:SUBLANES, :])
    rows = lax.broadcasted_iota(jnp.int32, (ts, 1), 0)
    x_shift = jnp.where(rows == 0, p_last, pltpu.roll(x, 1, 0))
    xx = x_shift - x
    mix = mix_ref[...]

    def xm(j):
        return x + xx * mix[j:j + 1]

    r = _bdot(xm(0), wr_ref[...])
    k = _bdot(xm(1), wk_ref[...])
    xv = xm(2)
    v = _bdot(xv, wv_ref[...])
    w_log = -jax.nn.softplus(-(w0_ref[...] + _bdot(jnp.tanh(_bdot(xm(3), w1_ref[...])), w2_ref[...]))) - 0.5
    decay = jnp.exp(-jnp.exp(w_log))
    if has_vres:
        v = v + (vf_ref[...] - v) * jax.nn.sigmoid(v0_ref[...] + _bdot(_bdot(xv, v1_ref[...]), v2_ref[...]))
    a = jax.nn.sigmoid(a0_ref[...] + _bdot(_bdot(xm(4), a1_ref[...]), a2_ref[...]))
    g = _bdot(jax.nn.sigmoid(_bdot(xm(5), g1_ref[...])), g2_ref[...])
    r_o[...] = r
    w_o[...] = decay
    k_o[...] = k
    v_o[...] = v
    a_o[...] = a
    g_o[...] = g


def _rwkv_proj(x2d, seq, mix, w_rkv, w0, w1, w2, a0, a1, a2, g1, g2, vres, ts=512):
    t, d = x2d.shape
    row = lambda v: v.reshape(1, d)
    tok = pl.BlockSpec((ts, d), lambda i: (i, 0))
    ins = [x2d, x2d, mix, w_rkv[0].astype(BF16), w_rkv[1].astype(BF16), w_rkv[2].astype(BF16),
           row(w0), w1.astype(BF16), w2.astype(BF16), row(a0), a1.astype(BF16), a2.astype(BF16),
           g1.astype(BF16), g2.astype(BF16)]
    specs = [_prev_rows_spec(ts, d), tok] + [_const_spec(a.shape) for a in ins[2:]]
    if vres is not None:
        v_first, v0, v1, v2 = vres
        extra = [v_first, row(v0), v1.astype(BF16), v2.astype(BF16)]
        ins += extra
        specs += [tok] + [_const_spec(a.shape) for a in extra[1:]]
    out_shape = [jax.ShapeDtypeStruct((t, d), F32)] * 6
    return pl.pallas_call(
        functools.partial(_rwkv_proj_kernel, ts=ts, seq=seq, has_vres=vres is not None),
        grid=(t // ts,),
        in_specs=specs,
        out_specs=[tok] * 6,
        out_shape=out_shape,
        compiler_params=pltpu.CompilerParams(dimension_semantics=("parallel",),
                                             vmem_limit_bytes=VMEM_LIMIT),
        name="rwkv_proj",
    )(*ins)


def _to_scan_kernel(x_ref, o_ref, zt_scr, w_scr, *, gb, d, n):
    heads = d // n
    per_tile = LANES // n
    for b in range(gb):
        for j in range(d // LANES):
            xt = x_ref[b, :, j * LANES:(j + 1) * LANES].T
            for hh in range(per_tile):
                h = j * per_tile + hh
                zt_scr[pl.ds((b * heads + h) * PAIR_PITCH, n), :] = xt[hh * n:(hh + 1) * n]
    tt = o_ref.shape[0]
    for c in range(n):
        w_scr[pl.ds(c * TIME_PITCH, tt), :] = zt_scr[pl.ds(c, gb * heads, stride=PAIR_PITCH), :].T
    for t in range(tt):
        o_ref[t] = w_scr[pl.ds(t, n, stride=TIME_PITCH), :]


def _to_scan(x2d, gb, seq):
    t, d = x2d.shape
    tt = LANES
    n = N_HEAD
    pairs = gb * d // n
    batch = t // seq
    return pl.pallas_call(
        functools.partial(_to_scan_kernel, gb=gb, d=d, n=n),
        grid=(batch // gb, seq // tt),
        in_specs=[pl.BlockSpec((gb, tt, d), lambda g, i: (g, i, 0))],
        out_specs=pl.BlockSpec((tt, n, pairs), lambda g, i: (i, 0, g)),
        out_shape=jax.ShapeDtypeStruct((seq, n, batch * d // n), F32),
        scratch_shapes=[pltpu.VMEM((pairs * PAIR_PITCH, tt), F32), pltpu.VMEM((n * TIME_PITCH, pairs), F32)],
        compiler_params=pltpu.CompilerParams(dimension_semantics=("parallel", "parallel"),
                                             vmem_limit_bytes=VMEM_LIMIT),
        name="to_scan_layout",
    )(x2d.reshape(batch, seq, d))


def _from_scan_kernel(y_ref, o_ref, zt_scr, w_scr, *, gb, d, n):
    heads = d // n
    tt = y_ref.shape[0]
    for t in range(tt):
        w_scr[pl.ds(t, n, stride=TIME_PITCH), :] = y_ref[t]
    for c in range(n):
        zt_scr[pl.ds(c, gb * heads, stride=PAIR_PITCH), :] = w_scr[pl.ds(c * TIME_PITCH, tt), :].T
    per_tile = LANES // n
    for b in range(gb):
        for j in range(d // LANES):
            xt = jnp.concatenate([zt_scr[pl.ds((b * heads + j * per_tile + hh) * PAIR_PITCH, n), :]
                                  for hh in range(per_tile)], axis=0)
            o_ref[b, :, j * LANES:(j + 1) * LANES] = xt.T


def _from_scan(y, gb, d):
    seq, n, all_pairs = y.shape
    tt = LANES
    pairs = gb * d // n
    batch = all_pairs * n // d
    return pl.pallas_call(
        functools.partial(_from_scan_kernel, gb=gb, d=d, n=n),
        grid=(batch // gb, seq // tt),
        in_specs=[pl.BlockSpec((tt, n, pairs), lambda g, i: (i, 0, g))],
        out_specs=pl.BlockSpec((gb, tt, d), lambda g, i: (g, i, 0)),
        out_shape=jax.ShapeDtypeStruct((batch, seq, d), F32),
        scratch_shapes=[pltpu.VMEM((pairs * PAIR_PITCH, tt), F32), pltpu.VMEM((n * TIME_PITCH, pairs), F32)],
        compiler_params=pltpu.CompilerParams(dimension_semantics=("parallel", "parallel"),
                                             vmem_limit_bytes=VMEM_LIMIT),
        name="from_scan_layout",
    )(y).reshape(batch * seq, d)


def _rows_of_sums(parts):
    row = lax.broadcasted_iota(jnp.int32, parts[0].shape, 0)
    dist = SUBLANES // 2
    while dist >= 1:
        lower = (row & dist) == 0
        half = len(parts) // 2
        parts = [jnp.where(lower, parts[j] + pltpu.roll(parts[j], SUBLANES - dist, 0),
                           parts[j + half] + pltpu.roll(parts[j + half], dist, 0)) for j in range(half)]
        dist //= 2
    return parts[0]


def _wkv_kernel(r_ref, w_ref, k_ref, v_ref, al_ref, kkp_ref, kap_ref, rk_ref, lxg_ref, lxb_ref, y_ref, s_scr,
                *, tc, n):
    @pl.when(pl.program_id(1) == 0)
    def _():
        s_scr[...] = jnp.zeros_like(s_scr)

    groups = n // SUBLANES
    inv_n = 1.0 / n

    def step(t, c_prev):
        k = k_ref[t]
        r = r_ref[t]
        lr = al_ref[t]
        kk = k * kkp_ref[...]
        kk = kk / jnp.maximum(jnp.sqrt(jnp.sum(kk * kk, axis=0, keepdims=True)), 1e-12)
        k = k * (1.0 + (lr - 1.0) * kap_ref[...])
        c_now = c_prev * w_ref[t]
        inv_c = 1.0 / c_now
        a = -kk * c_prev
        b = kk * lr
        b_s = b * inv_c
        k_s = k * inv_c
        wr = c_now * r
        br = jnp.sum(b * r, axis=0, keepdims=True)
        kr = jnp.sum(k * r, axis=0, keepdims=True)

        def vblock(vb, c):
            base = pl.multiple_of(vb * SUBLANES, SUBLANES)
            pa, py = [], []
            for j in range(SUBLANES):
                s = s_scr[base + j]
                pa.append(jnp.sum((s * a).reshape(groups, SUBLANES, LANES), axis=0))
                py.append(jnp.sum((s * wr).reshape(groups, SUBLANES, LANES), axis=0))
            sa = _rows_of_sums(pa)
            vv = v_ref[t, pl.ds(base, SUBLANES), :]
            y_ref[t, pl.ds(base, SUBLANES), :] = _rows_of_sums(py) + sa * br + vv * kr
            for j in range(SUBLANES):
                s_scr[base + j] = s_scr[base + j] + sa[j:j + 1] * b_s + vv[j:j + 1] * k_s
            return c

        lax.fori_loop(0, groups, vblock, 0, unroll=WKV_UNROLL)

        y = y_ref[t]
        yc = y - jnp.sum(y, axis=0, keepdims=True) * inv_n
        var = jnp.sum(yc * yc, axis=0, keepdims=True) * inv_n
        bonus = jnp.sum(r * k * rk_ref[...], axis=0, keepdims=True) * v_ref[t]
        y_ref[t] = yc * lax.rsqrt(var + GN_EPS) * lxg_ref[...] + lxb_ref[...] + bonus
        return c_now

    c_end = lax.fori_loop(0, tc, step, jnp.ones((n, LANES), F32))
    s_scr[...] = s_scr[...] * c_end[None]


def _wkv(r, w, k, v, lr, params, tc=WKV_STEPS):
    s, n, pairs = r.shape
    spec = pl.BlockSpec((tc, n, LANES), lambda p, c: (c, 0, p))
    pspec = pl.BlockSpec((n, LANES), lambda p, c: (0, p))
    return pl.pallas_call(
        functools.partial(_wkv_kernel, tc=tc, n=n),
        grid=(pairs // LANES, s // tc),
        in_specs=[spec] * 5 + [pspec] * len(params),
        out_specs=spec,
        out_shape=jax.ShapeDtypeStruct((s, n, pairs), F32),
        scratch_shapes=[pltpu.VMEM((n, n, LANES), F32)],
        compiler_params=pltpu.CompilerParams(dimension_semantics=("parallel", "arbitrary"),
                                             vmem_limit_bytes=VMEM_LIMIT),
        name="wkv7_scan",
    )(r, w, k, v, lr, *params)


def _rwkv_out_kernel(z_ref, g_ref, x_ref, wo_ref, lng_ref, lnb_ref, o_ref, *, alpha):
    h = _bdot(z_ref[...] * g_ref[...], wo_ref[...])
    o_ref[...] = _layer_norm(alpha * x_ref[...] + h, lng_ref[...], lnb_ref[...])


def _rwkv_out(z, g, x2d, w_o, ln_g, ln_b, alpha, ts=512):
    t, d = x2d.shape
    row = lambda a: a.reshape(1, d)
    tok = pl.BlockSpec((ts, d), lambda i: (i, 0))
    consts = [w_o.astype(BF16), row(ln_g), row(ln_b)]
    return pl.pallas_call(
        functools.partial(_rwkv_out_kernel, alpha=alpha),
        grid=(t // ts,),
        in_specs=[tok] * 3 + [_const_spec(a.shape) for a in consts],
        out_specs=tok,
        out_shape=jax.ShapeDtypeStruct((t, d), F32),
        compiler_params=pltpu.CompilerParams(dimension_semantics=("parallel",),
                                             vmem_limit_bytes=VMEM_LIMIT),
        name="rwkv_out",
    )(z, g, x2d, *consts)


def _ffn_kernel(xp_ref, x_ref, wup_ref, cw_ref, cb_ref, wdn_ref, lng_ref, lnb_ref, o_ref, u_scr,
                *, ts, seq, d_ff, fc, alpha):
    first = (pl.program_id(0) * ts) % seq == 0
    x = x_ref[...]
    xp = jnp.where(first, 0.0, xp_ref[...])
    xe = jnp.concatenate([xp, x], axis=0).astype(BF16)

    def conv_cols(off):
        u_scr[...] = jnp.dot(xe, wup_ref[:, off:off + fc], preferred_element_type=F32)
        cw = cw_ref[:, off:off + fc]
        return (u_scr[SUBLANES - 2:SUBLANES - 2 + ts, :] * cw[0:1]
                + u_scr[SUBLANES - 1:SUBLANES - 1 + ts, :] * cw[1:2]
                + u_scr[SUBLANES:SUBLANES + ts, :] * cw[2:3]
                + cb_ref[:, off:off + fc])

    acc = jnp.zeros(x.shape, F32)
    for c in range(d_ff // fc):
        gate = conv_cols(c * fc)
        val = conv_cols(d_ff + c * fc)
        act = jax.nn.silu(gate) * val
        acc = acc + jnp.dot(act.astype(BF16), wdn_ref[c * fc:(c + 1) * fc, :], preferred_element_type=F32)
    o_ref[...] = _layer_norm(alpha * x + acc, lng_ref[...], lnb_ref[...])


def _ffn(x2d, seq, w_up, conv_w, conv_b, w_down, ln_g, ln_b, alpha, ts=512):
    t, d = x2d.shape
    d_ff = w_down.shape[0]
    fc = d_ff
    tok = pl.BlockSpec((ts, d), lambda i: (i, 0))
    single = pl.Buffered(1)
    wup_spec = pl.BlockSpec((d, 2 * d_ff), lambda i: (0, 0), pipeline_mode=single)
    wdn_spec = pl.BlockSpec((d_ff, d), lambda i: (0, 0), pipeline_mode=single)
    consts = [conv_w, conv_b.reshape(1, -1)]
    rows = [ln_g.reshape(1, d), ln_b.reshape(1, d)]
    return pl.pallas_call(
        functools.partial(_ffn_kernel, ts=ts, seq=seq, d_ff=d_ff, fc=fc, alpha=alpha),
        grid=(t // ts,),
        in_specs=[_prev_rows_spec(ts, d), tok, wup_spec] + [_const_spec(a.shape) for a in consts]
                 + [wdn_spec] + [_const_spec(a.shape) for a in rows],
        out_specs=tok,
        out_shape=jax.ShapeDtypeStruct((t, d), F32),
        scratch_shapes=[pltpu.VMEM((ts + SUBLANES, fc), F32)],
        compiler_params=pltpu.CompilerParams(dimension_semantics=("parallel",),
                                             vmem_limit_bytes=VMEM_LIMIT),
        name="conv_ffn",
    )(x2d, x2d, w_up.astype(BF16), *consts, w_down.astype(BF16), *rows)


def _dsa_proj_kernel(x_ref, wcq_ref, wckv_ref, wki_ref, wwit_ref, qg_ref, kvg_ref, wuq_ref, wukp_ref,
                     wqi_ref, kig_ref, kib_ref, qa_o, qi_o, ki_o, wit_o, ckv_o, ckvt_o, *, cw):
    x = x_ref[...].astype(BF16)
    c_q = jnp.dot(x, wcq_ref[...], preferred_element_type=F32)
    c_kv = jnp.dot(x, wckv_ref[...], preferred_element_type=F32)
    k_idx = jnp.dot(x, wki_ref[...], preferred_element_type=F32)
    w_idx_t = _bdot_nt(wwit_ref[...], x)

    def rms(z, g):
        return z * lax.rsqrt(jnp.mean(z * z, axis=-1, keepdims=True) + 1e-6) * g

    c_q = rms(c_q, qg_ref[...]).astype(BF16)
    c_kv = rms(c_kv, kvg_ref[...])
    ckv_o[...] = c_kv.astype(BF16)
    for jc in range(c_kv.shape[0] // cw):
        ckvt_o[jc] = c_kv[jc * cw:(jc + 1) * cw].T.astype(BF16)
    q = jnp.dot(c_q, wuq_ref[...], preferred_element_type=F32)
    qk_scale = QK_HEAD ** -0.5 * LOG2E
    for hp in range(ATT_HEADS // 2):
        qa = _bdot(q[:, hp * LANES:(hp + 1) * LANES], wukp_ref[hp]) * qk_scale
        qa_o[2 * hp] = qa[:, :KV_LORA].astype(BF16)
        qa_o[2 * hp + 1] = qa[:, KV_LORA:].astype(BF16)
    for h in range(IDX_HEADS):
        qi_o[h] = jnp.dot(c_q, wqi_ref[h], preferred_element_type=F32).astype(BF16)
    ki_o[...] = _layer_norm(k_idx, kig_ref[...], kib_ref[...]).astype(BF16)
    wit_o[...] = w_idx_t * (IDX_HEADS ** -0.5 * IDX_DIM ** -0.5)


def _dsa_proj(x2d, w_in, q_norm_g, kv_norm_g, w_uq, w_uk, w_qidx, kidx_g, kidx_b, ts=512):
    t, d = x2d.shape
    cw = DSA_ROWS
    o1, o2, o3 = Q_LORA, Q_LORA + KV_LORA, Q_LORA + KV_LORA + IDX_DIM
    w_in = w_in.astype(BF16)
    z = jnp.zeros((ATT_HEADS // 2, QK_HEAD, KV_LORA), F32)
    wuk_pair = jnp.concatenate([jnp.concatenate([w_uk[0::2], z], axis=2),
                                jnp.concatenate([z, w_uk[1::2]], axis=2)], axis=1).astype(BF16)
    wqi = w_qidx.reshape(Q_LORA, IDX_HEADS, IDX_DIM).transpose(1, 0, 2).astype(BF16)
    consts = [w_in[:, :o1], w_in[:, o1:o2], w_in[:, o2:o3], w_in[:, o3:].T, q_norm_g.reshape(1, -1),
              kv_norm_g.reshape(1, -1), w_uq.astype(BF16), wuk_pair, wqi, kidx_g.reshape(1, -1),
              kidx_b.reshape(1, -1)]
    out_shape = [jax.ShapeDtypeStruct((ATT_HEADS, t, KV_LORA), BF16),
                 jax.ShapeDtypeStruct((IDX_HEADS, t, IDX_DIM), BF16),
                 jax.ShapeDtypeStruct((t, IDX_DIM), BF16),
                 jax.ShapeDtypeStruct((IDX_HEADS, t), F32),
                 jax.ShapeDtypeStruct((t, KV_LORA), BF16),
                 jax.ShapeDtypeStruct((t // cw, KV_LORA, cw), BF16)]
    out_specs = [pl.BlockSpec((ATT_HEADS, ts, KV_LORA), lambda i: (0, i, 0)),
                 pl.BlockSpec((IDX_HEADS, ts, IDX_DIM), lambda i: (0, i, 0)),
                 pl.BlockSpec((ts, IDX_DIM), lambda i: (i, 0)),
                 pl.BlockSpec((IDX_HEADS, ts), lambda i: (0, i)),
                 pl.BlockSpec((ts, KV_LORA), lambda i: (i, 0)),
                 pl.BlockSpec((ts // cw, KV_LORA, cw), lambda i: (i, 0, 0))]
    return pl.pallas_call(
        functools.partial(_dsa_proj_kernel, cw=cw),
        grid=(t // ts,),
        in_specs=[pl.BlockSpec((ts, d), lambda i: (i, 0))] + [_const_spec(a.shape) for a in consts],
        out_specs=out_specs,
        out_shape=out_shape,
        compiler_params=pltpu.CompilerParams(dimension_semantics=("parallel",),
                                             vmem_limit_bytes=VMEM_LIMIT),
        name="dsa_proj",
    )(x2d, *consts)


def _bias_tiles_kernel(rb_ref, o_ref, *, qb):
    c = lax.broadcasted_iota(jnp.int32, (2 * qb, qb), 0)
    tl = lax.broadcasted_iota(jnp.int32, (2 * qb, qb), 1)
    n = jnp.maximum(qb + tl - c, 0)
    max_exact = REL_BUCKETS // 2
    nf = jnp.maximum(n, 1).astype(F32)
    large = max_exact + (jnp.log(nf / max_exact) / math.log(REL_MAX_DIST / max_exact)
                         * (REL_BUCKETS - max_exact)).astype(jnp.int32)
    large = jnp.minimum(large, REL_BUCKETS - 1)
    bucket = jnp.where(n < max_exact, n, large)
    for h in range(ATT_HEADS):
        acc = jnp.zeros((2 * qb, qb), F32)
        for bkt in range(REL_BUCKETS):
            acc = jnp.where(bucket == bkt, rb_ref[bkt, h], acc)
        o_ref[h] = (acc - rb_ref[REL_BUCKETS - 1, h]) * LOG2E


def _bias_tiles(rel_bias, qb):
    return pl.pallas_call(
        functools.partial(_bias_tiles_kernel, qb=qb),
        in_specs=[pl.BlockSpec(memory_space=pltpu.SMEM)],
        out_specs=pl.BlockSpec(memory_space=pltpu.VMEM),
        out_shape=jax.ShapeDtypeStruct((ATT_HEADS, 2 * qb, qb), F32),
        name="rel_bias_tiles",
    )(rel_bias)


def _dsa_attn_kernel(qi_ref, wit_ref, ki_ref, ckv_ref, ckvt_ref, qa_ref, near_ref, tril_ref, wuvt_ref,
                     wo_ref, x_ref, lng_ref, lnb_ref, o_ref, key_scr, khi_scr, klo_scr, madd_scr, m_scr, den_scr, acc_scr,
                     sc_scr, p_scr,
                     *, topk, qb, alpha):
    cw = qb
    i = pl.program_id(1)
    nch = i + 1
    t_pos = i * qb + lax.broadcasted_iota(jnp.int32, (1, qb), 1)
    s_loc = lax.broadcasted_iota(jnp.int32, (cw, 1), 0)
    int_min = jnp.int32(INT_MIN)

    qi_all = qi_ref[...].reshape(IDX_HEADS * qb, IDX_DIM)
    wit = wit_ref[...]

    def score_chunk(c, carry):
        off = pl.multiple_of(c * cw, cw)
        s_all = _bdot_nt(ki_ref[pl.ds(off, cw), :], qi_all)
        score = jnp.zeros((cw, qb), F32)
        for h in range(IDX_HEADS):
            score = score + jnp.maximum(s_all[:, h * qb:(h + 1) * qb], 0.0) * wit[h:h + 1]
        bits = lax.bitcast_convert_type(score, jnp.int32)
        key = jnp.where(bits < 0, bits ^ jnp.int32(0x7FFFFFFF), bits)
        key = jnp.where(off + s_loc <= t_pos, key, int_min)
        key_scr[c] = key
        khi_scr[c] = lax.shift_right_arithmetic(key, 16).astype(jnp.int16)
        klo_scr[c] = ((key & 0xFFFF) - 32768).astype(jnp.int16)
        return carry

    lax.fori_loop(0, nch, score_chunk, 0)

    def count(pred):
        def body(c, acc):
            kc = key_scr[c]
            for r in range(cw // SUBLANES):
                acc = acc + jnp.where(pred(kc[r * SUBLANES:(r + 1) * SUBLANES]), 1.0, 0.0)
            return acc

        acc = lax.fori_loop(0, nch, body, jnp.zeros((SUBLANES, qb), F32))
        return jnp.sum(acc, axis=0, keepdims=True)

    rows16 = 2 * SUBLANES
    one16, zero16, min16 = jnp.int16(1), jnp.int16(0), jnp.int16(-32768)

    def count16(scr, pred):
        def body(c, acc):
            kc = scr[c]
            for r in range(cw // rows16):
                acc = acc + jnp.where(pred(kc[r * rows16:(r + 1) * rows16]), one16, zero16)
            return acc

        acc = lax.fori_loop(0, nch, body, jnp.zeros((rows16, qb), jnp.int16))
        return jnp.sum(acc.astype(jnp.int32), axis=0, keepdims=True)

    def search16(scr, want):
        tau = jnp.where(count16(scr, lambda x: x >= zero16) >= want, jnp.int32(0), jnp.int32(-32768))

        def bit_step(bi, tau):
            cand = tau | lax.shift_left(jnp.int32(1), jnp.int32(14) - bi)
            c16 = cand.astype(jnp.int16)
            return jnp.where(count16(scr, lambda x: x >= c16) >= want, cand, tau)

        return lax.fori_loop(0, 15, bit_step, tau)

    tau_hi = search16(khi_scr, jnp.int32(topk))
    th16 = tau_hi.astype(jnp.int16)
    want_lo = topk - count16(khi_scr, lambda x: x > th16)

    def mask_chunk(c, carry):
        klo_scr[c] = jnp.where(khi_scr[c] == th16, klo_scr[c], min16)
        return carry

    lax.fori_loop(0, nch, mask_chunk, 0)
    tau_lo = search16(klo_scr, want_lo)
    tau = lax.shift_left(tau_hi, 16) | ((tau_lo + 32768) & 0xFFFF)
    kf = float(topk)

    need = kf - count(lambda kc: kc > tau)
    has_kth = tau > int_min
    tril = tril_ref[...]

    def select_chunk(c, run):
        kc = key_scr[c]
        eq = jnp.logical_and(kc == tau, has_kth)
        e = jnp.where(eq, 1.0, 0.0)
        rank = jnp.dot(tril, e.astype(BF16), preferred_element_type=F32) + run
        sel = jnp.logical_or(kc > tau, jnp.logical_and(eq, rank <= need))
        madd_scr[c] = jnp.where(sel, 0.0, NEG_MASK)
        return run + jnp.sum(e, axis=0, keepdims=True)

    lax.fori_loop(0, nch, select_chunk, jnp.zeros((1, qb), F32))

    m_scr[...] = jnp.full(m_scr.shape, -jnp.inf, F32)
    den_scr[...] = jnp.zeros(den_scr.shape, F32)
    acc_scr[...] = jnp.zeros(acc_scr.shape, F32)
    qa_all = qa_ref[...].reshape(ATT_HEADS * qb, KV_LORA)
    halves = qb // LANES

    def att_chunk(c, with_bias):
        off = pl.multiple_of(c * cw, cw)
        ckv_c = ckv_ref[pl.ds(off, cw), :]
        boff = pl.multiple_of(jnp.where(c == i, qb, 0), qb)
        logits = _bdot_nt(ckv_c, qa_all)
        for g in range(ATT_HEADS * halves):
            cols = slice(g * LANES, (g + 1) * LANES)
            qcols = slice((g % halves) * LANES, (g % halves + 1) * LANES)
            l = logits[:, cols] + madd_scr[c, :, qcols]
            if with_bias:
                l = l + near_ref[g // halves, pl.ds(boff, qb), qcols]
            m_old = m_scr[:, cols]
            m_new = jnp.maximum(m_old, jnp.max(l, axis=0, keepdims=True))
            p = jnp.exp2(l - m_new)
            scale = jnp.exp2(m_old - m_new)
            den_scr[:, cols] = den_scr[:, cols] * scale + jnp.sum(p, axis=0, keepdims=True)
            m_scr[:, cols] = m_new
            sc_scr[:, cols] = scale
            p_scr[:, cols] = p.astype(BF16)
        pv = jnp.dot(ckvt_ref[c], p_scr[...], preferred_element_type=F32)
        acc_scr[...] = acc_scr[...] * sc_scr[...] + pv

    def far_body(c, carry):
        att_chunk(c, False)
        return carry

    def near_body(c, carry):
        att_chunk(c, True)
        return carry

    c_near = jnp.maximum(i - 1, 0)
    lax.fori_loop(0, c_near, far_body, 0)
    lax.fori_loop(c_near, nch, near_body, 0)

    o_rows = []
    for h in range(ATT_HEADS):
        o_lat_t = acc_scr[:, h * qb:(h + 1) * qb] / den_scr[:, h * qb:(h + 1) * qb]
        o_rows.append(_bdot(wuvt_ref[h], o_lat_t))
    h_out = _bdot(jnp.concatenate(o_rows, axis=0).T, wo_ref[...])
    o_ref[...] = _layer_norm(alpha * x_ref[...] + h_out, lng_ref[...], lnb_ref[...])


def _dsa_attn(x2d, batch, seq, qa, qi, ki, wit, ckv, ckvt, near, w_uv, w_o, ln_g, ln_b, alpha):
    t, d = x2d.shape
    qb = DSA_ROWS
    assert KV_LORA == LANES
    nb = seq // qb
    topk = min(TOPK_MAX, seq // 4)
    tril = (jnp.arange(qb)[:, None] >= jnp.arange(qb)[None, :]).astype(BF16)
    wuv_t = w_uv.transpose(0, 2, 1).astype(BF16)
    wo = w_o.astype(BF16)
    tok = lambda b, i: (b * nb + i, 0)
    in_specs = [
        pl.BlockSpec((IDX_HEADS, qb, IDX_DIM), lambda b, i: (0, b * nb + i, 0)),
        pl.BlockSpec((IDX_HEADS, qb), lambda b, i: (0, b * nb + i)),
        pl.BlockSpec((seq, IDX_DIM), lambda b, i: (b, 0)),
        pl.BlockSpec((seq, KV_LORA), lambda b, i: (b, 0)),
        pl.BlockSpec((nb, KV_LORA, qb), lambda b, i: (b, 0, 0)),
        pl.BlockSpec((ATT_HEADS, qb, KV_LORA), lambda b, i: (0, b * nb + i, 0)),
        pl.BlockSpec(near.shape, lambda b, i: (0, 0, 0), pipeline_mode=pl.Buffered(1)),
        _const_spec(tril.shape),
        _const_spec(wuv_t.shape),
        _const_spec(wo.shape),
        pl.BlockSpec((qb, d), tok),
        _const_spec((1, d)),
        _const_spec((1, d)),
    ]
    scratch = [pltpu.VMEM((nb, qb, qb), jnp.int32),
               pltpu.VMEM((nb, qb, qb), jnp.int16),
               pltpu.VMEM((nb, qb, qb), jnp.int16),
               pltpu.VMEM((nb, qb, qb), F32),
               pltpu.VMEM((1, ATT_HEADS * qb), F32),
               pltpu.VMEM((1, ATT_HEADS * qb), F32),
               pltpu.VMEM((KV_LORA, ATT_HEADS * qb), F32),
               pltpu.VMEM((1, ATT_HEADS * qb), F32),
               pltpu.VMEM((qb, ATT_HEADS * qb), BF16)]
    return pl.pallas_call(
        functools.partial(_dsa_attn_kernel, topk=topk, qb=qb, alpha=alpha),
        grid=(batch, nb),
        in_specs=in_specs,
        out_specs=pl.BlockSpec((qb, d), tok),
        out_shape=jax.ShapeDtypeStruct((t, d), F32),
        scratch_shapes=scratch,
        compiler_params=pltpu.CompilerParams(dimension_semantics=("parallel", "parallel"),
                                             vmem_limit_bytes=VMEM_LIMIT),
        name="dsa_attn",
    )(qi, wit, ki, ckv, ckvt, qa, near, tril, wuv_t, wo, x2d, ln_g.reshape(1, d), ln_b.reshape(1, d))


def kernel(x, ln_g, ln_b, rwkv_mix, rwkv_w_rkv, rwkv_w0, rwkv_w1, rwkv_w2, rwkv_a0, rwkv_a1, rwkv_a2, rwkv_v0, rwkv_v1, rwkv_v2, rwkv_g1, rwkv_g2, rwkv_k_k, rwkv_k_a, rwkv_r_k, rwkv_lnx_g, rwkv_lnx_b, rwkv_w_o, dsa_w_in, dsa_q_norm_g, dsa_kv_norm_g, dsa_w_uq, dsa_w_uk, dsa_w_uv, dsa_w_qidx, dsa_kidx_g, dsa_kidx_b, dsa_w_o, rel_bias, ffn_w_up, ffn_conv_w, ffn_conv_b, ffn_w_down):
    batch, seq, d = x.shape
    depth = ln_g.shape[0]
    heads = d // N_HEAD
    alpha = (2 * depth) ** 0.25
    gb = LANES // heads
    t = batch * seq
    x2d = x.reshape(t, d)

    def scan_param(v):
        return jnp.tile(v.reshape(heads, N_HEAD).T, (1, batch))

    near = _bias_tiles(rel_bias, DSA_ROWS)
    to_scan = functools.partial(_to_scan, gb=gb, seq=seq)

    v_first = None
    for i in range(depth):
        j = i // 2
        if i % 2 == 0:
            vres = None if j == 0 else (v_first, rwkv_v0[j - 1], rwkv_v1[j - 1], rwkv_v2[j - 1])
            r, w, k, v, lr, gate = _rwkv_proj(
                x2d, seq, rwkv_mix[j], rwkv_w_rkv[j], rwkv_w0[j], rwkv_w1[j], rwkv_w2[j], rwkv_a0[j],
                rwkv_a1[j], rwkv_a2[j], rwkv_g1[j], rwkv_g2[j], vres)
            if j == 0:
                v_first = v
            params = [scan_param(q) for q in (rwkv_k_k[j], rwkv_k_a[j], rwkv_r_k[j].reshape(-1),
                                              rwkv_lnx_g[j], rwkv_lnx_b[j])]
            z = _wkv(to_scan(r), to_scan(w), to_scan(k), to_scan(v), to_scan(lr), params)
            x2d = _rwkv_out(_from_scan(z, gb, d), gate, x2d, rwkv_w_o[j], ln_g[i, 0], ln_b[i, 0], alpha)
        else:
            qa, qi, ki, wit, ckv, ckvt = _dsa_proj(x2d, dsa_w_in[j], dsa_q_norm_g[j], dsa_kv_norm_g[j],
                                                   dsa_w_uq[j], dsa_w_uk[j], dsa_w_qidx[j], dsa_kidx_g[j],
                                                   dsa_kidx_b[j])
            x2d = _dsa_attn(x2d, batch, seq, qa, qi, ki, wit, ckv, ckvt, near, dsa_w_uv[j], dsa_w_o[j],
                            ln_g[i, 0], ln_b[i, 0], alpha)
        x2d = _ffn(x2d, seq, ffn_w_up[i], ffn_conv_w[i], ffn_conv_b[i], ffn_w_down[i],
                   ln_g[i, 1], ln_b[i, 1], alpha)
    return x2d.reshape(batch, seq, d)
```

```python
import functools
import math

import jax
import jax.numpy as jnp
from jax import lax
from jax.experimental import pallas as pl
from jax.experimental.pallas import tpu as pltpu

F32 = jnp.float32
BF16 = jnp.bfloat16

N_HEAD = 64
ATT_HEADS = 16
QK_HEAD = 64
V_HEAD = 64
Q_LORA = 256
KV_LORA = 128
IDX_HEADS = 8
IDX_DIM = 64
TOPK_MAX = 256
DSA_ROWS = 256
REL_BUCKETS = 32
REL_MAX_DIST = 128
CONV_W = 3
GN_EPS = 64e-5
LN_EPS = 1e-5

LANES = 128
SUBLANES = 8
VMEM_LIMIT = 56 * 1024 * 1024
PAIR_PITCH = N_HEAD + 8
TIME_PITCH = LANES + 8
WKV_STEPS = 32
WKV_UNROLL = 8

NEG_MASK = -1e30
LOG2E = math.log2(math.e)
INT_MIN = -2 ** 31


def _bdot(a, b):
    return jnp.dot(a.astype(BF16), b.astype(BF16), preferred_element_type=F32)


def _bdot_nt(a, b):
    return lax.dot_general(a.astype(BF16), b.astype(BF16), (((1,), (1,)), ((), ())),
                           preferred_element_type=F32)


def _layer_norm(x, g, b):
    mu = jnp.mean(x, axis=-1, keepdims=True)
    xc = x - mu
    var = jnp.mean(xc * xc, axis=-1, keepdims=True)
    return xc * lax.rsqrt(var + LN_EPS) * g + b


def _const_spec(shape):
    nd = len(shape)
    return pl.BlockSpec(shape, lambda *_: (0,) * nd)


def _prev_rows_spec(ts, d):
    return pl.BlockSpec((SUBLANES, d), lambda i: (jnp.maximum(i * (ts // SUBLANES) - 1, 0), 0))


def _rwkv_proj_kernel(*refs, ts, seq, has_vres):
    (xp_ref, x_ref, mix_ref, wr_ref, wk_ref, wv_ref, w0_ref, w1_ref, w2_ref, a0_ref, a1_ref,
     a2_ref, g1_ref, g2_ref) = refs[:14]
    if has_vres:
        vf_ref, v0_ref, v1_ref, v2_ref = refs[14:18]
        outs = refs[18:]
    else:
        outs = refs[14:]
    r_o, w_o, k_o, v_o, a_o, g_o = outs

    first = (pl.program_id(0) * ts) % seq == 0
    x = x_ref[...]
    p_last = jnp.where(first, 0.0, xp_ref[SUBLANES - 1:SUBLANES, :])
    rows = lax.broadcasted_iota(jnp.int32, (ts, 1), 0)
    x_shift = jnp.where(rows == 0, p_last, pltpu.roll(x, 1, 0))
    xx = x_shift - x
    mix = mix_ref[...]

    def xm(j):
        return x + xx * mix[j:j + 1]

    r = _bdot(xm(0), wr_ref[...])
    k = _bdot(xm(1), wk_ref[...])
    xv = xm(2)
    v = _bdot(xv, wv_ref[...])
    w_log = -jax.nn.softplus(-(w0_ref[...] + _bdot(jnp.tanh(_bdot(xm(3), w1_ref[...])), w2_ref[...]))) - 0.5
    decay = jnp.exp(-jnp.exp(w_log))
    if has_vres:
        v = v + (vf_ref[...] - v) * jax.nn.sigmoid(v0_ref[...] + _bdot(_bdot(xv, v1_ref[...]), v2_ref[...]))
    a = jax.nn.sigmoid(a0_ref[...] + _bdot(_bdot(xm(4), a1_ref[...]), a2_ref[...]))
    g = _bdot(jax.nn.sigmoid(_bdot(xm(5), g1_ref[...])), g2_ref[...])
    r_o[...] = r
    w_o[...] = decay
    k_o[...] = k
    v_o[...] = v
    a_o[...] = a
    g_o[...] = g


def _rwkv_proj(x2d, seq, mix, w_rkv, w0, w1, w2, a0, a1, a2, g1, g2, vres, ts=512):
    t, d = x2d.shape
    row = lambda v: v.reshape(1, d)
    tok = pl.BlockSpec((ts, d), lambda i: (i, 0))
    ins = [x2d, x2d, mix, w_rkv[0].astype(BF16), w_rkv[1].astype(BF16), w_rkv[2].astype(BF16),
           row(w0), w1.astype(BF16), w2.astype(BF16), row(a0), a1.astype(BF16), a2.astype(BF16),
           g1.astype(BF16), g2.astype(BF16)]
    specs = [_prev_rows_spec(ts, d), tok] + [_const_spec(a.shape) for a in ins[2:]]
    if vres is not None:
        v_first, v0, v1, v2 = vres
        extra = [v_first, row(v0), v1.astype(BF16), v2.astype(BF16)]
        ins += extra
        specs += [tok] + [_const_spec(a.shape) for a in extra[1:]]
    out_shape = [jax.ShapeDtypeStruct((t, d), F32)] * 6
    return pl.pallas_call(
        functools.partial(_rwkv_proj_kernel, ts=ts, seq=seq, has_vres=vres is not None),
        grid=(t // ts,),
        in_specs=specs,
        out_specs=[tok] * 6,
        out_shape=out_shape,
        compiler_params=pltpu.CompilerParams(dimension_semantics=("parallel",),
                                             vmem_limit_bytes=VMEM_LIMIT),
        name="rwkv_proj",
    )(*ins)


def _to_scan_kernel(x_ref, o_ref, zt_scr, w_scr, *, gb, d, n):
    heads = d // n
    per_tile = LANES // n
    for b in range(gb):
        for j in range(d // LANES):
            xt = x_ref[b, :, j * LANES:(j + 1) * LANES].T
            for hh in range(per_tile):
                h = j * per_tile + hh
                zt_scr[pl.ds((b * heads + h) * PAIR_PITCH, n), :] = xt[hh * n:(hh + 1) * n]
    tt = o_ref.shape[0]
    for c in range(n):
        w_scr[pl.ds(c * TIME_PITCH, tt), :] = zt_scr[pl.ds(c, gb * heads, stride=PAIR_PITCH), :].T
    for t in range(tt):
        o_ref[t] = w_scr[pl.ds(t, n, stride=TIME_PITCH), :]


def _to_scan(x2d, gb, seq):
    t, d = x2d.shape
    tt = LANES
    n = N_HEAD
    pairs = gb * d // n
    batch = t // seq
    return pl.pallas_call(
        functools.partial(_to_scan_kernel, gb=gb, d=d, n=n),
        grid=(batch // gb, seq // tt),
        in_specs=[pl.BlockSpec((gb, tt, d), lambda g, i: (g, i, 0))],
        out_specs=pl.BlockSpec((tt, n, pairs), lambda g, i: (i, 0, g)),
        out_shape=jax.ShapeDtypeStruct((seq, n, batch * d // n), F32),
        scratch_shapes=[pltpu.VMEM((pairs * PAIR_PITCH, tt), F32), pltpu.VMEM((n * TIME_PITCH, pairs), F32)],
        compiler_params=pltpu.CompilerParams(dimension_semantics=("parallel", "parallel"),
                                             vmem_limit_bytes=VMEM_LIMIT),
        name="to_scan_layout",
    )(x2d.reshape(batch, seq, d))


def _from_scan_kernel(y_ref, o_ref, zt_scr, w_scr, *, gb, d, n):
    heads = d // n
    tt = y_ref.shape[0]
    for t in range(tt):
        w_scr[pl.ds(t, n, stride=TIME_PITCH), :] = y_ref[t]
    for c in range(n):
        zt_scr[pl.ds(c, gb * heads, stride=PAIR_PITCH), :] = w_scr[pl.ds(c * TIME_PITCH, tt), :].T
    per_tile = LANES // n
    for b in range(gb):
        for j in range(d // LANES):
            xt = jnp.concatenate([zt_scr[pl.ds((b * heads + j * per_tile + hh) * PAIR_PITCH, n), :]
                                  for hh in range(per_tile)], axis=0)
            o_ref[b, :, j * LANES:(j + 1) * LANES] = xt.T


def _from_scan(y, gb, d):
    seq, n, all_pairs = y.shape
    tt = LANES
    pairs = gb * d // n
    batch = all_pairs * n // d
    return pl.pallas_call(
        functools.partial(_from_scan_kernel, gb=gb, d=d, n=n),
        grid=(batch // gb, seq // tt),
        in_specs=[pl.BlockSpec((tt, n, pairs), lambda g, i: (i, 0, g))],
        out_specs=pl.BlockSpec((gb, tt, d), lambda g, i: (g, i, 0)),
        out_shape=jax.ShapeDtypeStruct((batch, seq, d), F32),
        scratch_shapes=[pltpu.VMEM((pairs * PAIR_PITCH, tt), F32), pltpu.VMEM((n * TIME_PITCH, pairs), F32)],
        compiler_params=pltpu.CompilerParams(dimension_semantics=("parallel", "parallel"),
                                             vmem_limit_bytes=VMEM_LIMIT),
        name="from_scan_layout",
    )(y).reshape(batch * seq, d)


def _rows_of_sums(parts):
    row = lax.broadcasted_iota(jnp.int32, parts[0].shape, 0)
    dist = SUBLANES // 2
    while dist >= 1:
        lower = (row & dist) == 0
        half = len(parts) // 2
        parts = [jnp.where(lower, parts[j] + pltpu.roll(parts[j], SUBLANES - dist, 0),
                           parts[j + half] + pltpu.roll(parts[j + half], dist, 0)) for j in range(half)]
        dist //= 2
    return parts[0]


def _wkv_kernel(r_ref, w_ref, k_ref, v_ref, al_ref, kkp_ref, kap_ref, rk_ref, lxg_ref, lxb_ref, y_ref, s_scr,
                *, tc, n):
    @pl.when(pl.program_id(1) == 0)
    def _():
        s_scr[...] = jnp.zeros_like(s_scr)

    groups = n // SUBLANES
    inv_n = 1.0 / n

    def step(t, c_prev):
        k = k_ref[t]
        r = r_ref[t]
        lr = al_ref[t]
        kk = k * kkp_ref[...]
        kk = kk / jnp.maximum(jnp.sqrt(jnp.sum(kk * kk, axis=0, keepdims=True)), 1e-12)
        k = k * (1.0 + (lr - 1.0) * kap_ref[...])
        c_now = c_prev * w_ref[t]
        inv_c = 1.0 / c_now
        a = -kk * c_prev
        b = kk * lr
        b_s = b * inv_c
        k_s = k * inv_c
        wr = c_now * r
        br = jnp.sum(b * r, axis=0, keepdims=True)
        kr = jnp.sum(k * r, axis=0, keepdims=True)

        def vblock(vb, c):
            base = pl.multiple_of(vb * SUBLANES, SUBLANES)
            pa, py = [], []
            for j in range(SUBLANES):
                s = s_scr[base + j]
                pa.append(jnp.sum((s * a).reshape(groups, SUBLANES, LANES), axis=0))
                py.append(jnp.sum((s * wr).reshape(groups, SUBLANES, LANES), axis=0))
            sa = _rows_of_sums(pa)
            vv = v_ref[t, pl.ds(base, SUBLANES), :]
            y_ref[t, pl.ds(base, SUBLANES), :] = _rows_of_sums(py) + sa * br + vv * kr
            for j in range(SUBLANES):
                s_scr[base + j] = s_scr[base + j] + sa[j:j + 1] * b_s + vv[j:j + 1] * k_s
            return c

        lax.fori_loop(0, groups, vblock, 0, unroll=WKV_UNROLL)

        y = y_ref[t]
        yc = y - jnp.sum(y, axis=0, keepdims=True) * inv_n
        var = jnp.sum(yc * yc, axis=0, keepdims=True) * inv_n
        bonus = jnp.sum(r * k * rk_ref[...], axis=0, keepdims=True) * v_ref[t]
        y_ref[t] = yc * lax.rsqrt(var + GN_EPS) * lxg_ref[...] + lxb_ref[...] + bonus
        return c_now

    c_end = lax.fori_loop(0, tc, step, jnp.ones((n, LANES), F32))
    s_scr[...] = s_scr[...] * c_end[None]


def _wkv(r, w, k, v, lr, params, tc=WKV_STEPS):
    s, n, pairs = r.shape
    spec = pl.BlockSpec((tc, n, LANES), lambda p, c: (c, 0, p))
    pspec = pl.BlockSpec((n, LANES), lambda p, c: (0, p))
    return pl.pallas_call(
        functools.partial(_wkv_kernel, tc=tc, n=n),
        grid=(pairs // LANES, s // tc),
        in_specs=[spec] * 5 + [pspec] * len(params),
        out_specs=spec,
        out_shape=jax.ShapeDtypeStruct((s, n, pairs), F32),
        scratch_shapes=[pltpu.VMEM((n, n, LANES), F32)],
        compiler_params=pltpu.CompilerParams(dimension_semantics=("parallel", "arbitrary"),
                                             vmem_limit_bytes=VMEM_LIMIT),
        name="wkv7_scan",
    )(r, w, k, v, lr, *params)


def _rwkv_out_kernel(z_ref, g_ref, x_ref, wo_ref, lng_ref, lnb_ref, o_ref, *, alpha):
    h = _bdot(z_ref[...] * g_ref[...], wo_ref[...])
    o_ref[...] = _layer_norm(alpha * x_ref[...] + h, lng_ref[...], lnb_ref[...])


def _rwkv_out(z, g, x2d, w_o, ln_g, ln_b, alpha, ts=512):
    t, d = x2d.shape
    row = lambda a: a.reshape(1, d)
    tok = pl.BlockSpec((ts, d), lambda i: (i, 0))
    consts = [w_o.astype(BF16), row(ln_g), row(ln_b)]
    return pl.pallas_call(
        functools.partial(_rwkv_out_kernel, alpha=alpha),
        grid=(t // ts,),
        in_specs=[tok] * 3 + [_const_spec(a.shape) for a in consts],
        out_specs=tok,
        out_shape=jax.ShapeDtypeStruct((t, d), F32),
        compiler_params=pltpu.CompilerParams(dimension_semantics=("parallel",),
                                             vmem_limit_bytes=VMEM_LIMIT),
        name="rwkv_out",
    )(z, g, x2d, *consts)


def _ffn_kernel(xp_ref, x_ref, wup_ref, cw_ref, cb_ref, wdn_ref, lng_ref, lnb_ref, o_ref, u_scr,
                *, ts, seq, d_ff, fc, alpha):
    first = (pl.program_id(0) * ts) % seq == 0
    x = x_ref[...]
    xp = jnp.where(first, 0.0, xp_ref[...])
    xe = jnp.concatenate([xp, x], axis=0).astype(BF16)

    def conv_cols(off):
        u_scr[...] = jnp.dot(xe, wup_ref[:, off:off + fc], preferred_element_type=F32)
        cw = cw_ref[:, off:off + fc]
        return (u_scr[SUBLANES - 2:SUBLANES - 2 + ts, :] * cw[0:1]
                + u_scr[SUBLANES - 1:SUBLANES - 1 + ts, :] * cw[1:2]
                + u_scr[SUBLANES:SUBLANES + ts, :] * cw[2:3]
                + cb_ref[:, off:off + fc])

    acc = jnp.zeros(x.shape, F32)
    for c in range(d_ff // fc):
        gate = conv_cols(c * fc)
        val = conv_cols(d_ff + c * fc)
        act = jax.nn.silu(gate) * val
        acc = acc + jnp.dot(act.astype(BF16), wdn_ref[c * fc:(c + 1) * fc, :], preferred_element_type=F32)
    o_ref[...] = _layer_norm(alpha * x + acc, lng_ref[...], lnb_ref[...])


def _ffn(x2d, seq, w_up, conv_w, conv_b, w_down, ln_g, ln_b, alpha, ts=512):
    t, d = x2d.shape
    d_ff = w_down.shape[0]
    fc = d_ff
    tok = pl.BlockSpec((ts, d), lambda i: (i, 0))
    single = pl.Buffered(1)
    wup_spec = pl.BlockSpec((d, 2 * d_ff), lambda i: (0, 0), pipeline_mode=single)
    wdn_spec = pl.BlockSpec((d_ff, d), lambda i: (0, 0), pipeline_mode=single)
    consts = [conv_w, conv_b.reshape(1, -1)]
    rows = [ln_g.reshape(1, d), ln_b.reshape(1, d)]
    return pl.pallas_call(
        functools.partial(_ffn_kernel, ts=ts, seq=seq, d_ff=d_ff, fc=fc, alpha=alpha),
        grid=(t // ts,),
        in_specs=[_prev_rows_spec(ts, d), tok, wup_spec] + [_const_spec(a.shape) for a in consts]
                 + [wdn_spec] + [_const_spec(a.shape) for a in rows],
        out_specs=tok,
        out_shape=jax.ShapeDtypeStruct((t, d), F32),
        scratch_shapes=[pltpu.VMEM((ts + SUBLANES, fc), F32)],
        compiler_params=pltpu.CompilerParams(dimension_semantics=("parallel",),
                                             vmem_limit_bytes=VMEM_LIMIT),
        name="conv_ffn",
    )(x2d, x2d, w_up.astype(BF16), *consts, w_down.astype(BF16), *rows)


def _dsa_proj_kernel(x_ref, wcq_ref, wckv_ref, wki_ref, wwit_ref, qg_ref, kvg_ref, wuq_ref, wukp_ref,
                     wqi_ref, kig_ref, kib_ref, qa_o, qi_o, ki_o, wit_o, ckv_o, ckvt_o, *, cw):
    x = x_ref[...].astype(BF16)
    c_q = jnp.dot(x, wcq_ref[...], preferred_element_type=F32)
    c_kv = jnp.dot(x, wckv_ref[...], preferred_element_type=F32)
    k_idx = jnp.dot(x, wki_ref[...], preferred_element_type=F32)
    w_idx_t = _bdot_nt(wwit_ref[...], x)

    def rms(z, g):
        return z * lax.rsqrt(jnp.mean(z * z, axis=-1, keepdims=True) + 1e-6) * g

    c_q = rms(c_q, qg_ref[...]).astype(BF16)
    c_kv = rms(c_kv, kvg_ref[...])
    ckv_o[...] = c_kv.astype(BF16)
    for jc in range(c_kv.shape[0] // cw):
        ckvt_o[jc] = c_kv[jc * cw:(jc + 1) * cw].T.astype(BF16)
    q = jnp.dot(c_q, wuq_ref[...], preferred_element_type=F32)
    qk_scale = QK_HEAD ** -0.5 * LOG2E
    for hp in range(ATT_HEADS // 2):
        qa = _bdot(q[:, hp * LANES:(hp + 1) * LANES], wukp_ref[hp]) * qk_scale
        qa_o[2 * hp] = qa[:, :KV_LORA].astype(BF16)
        qa_o[2 * hp + 1] = qa[:, KV_LORA:].astype(BF16)
    for h in range(IDX_HEADS):
        qi_o[h] = jnp.dot(c_q, wqi_ref[h], preferred_element_type=F32).astype(BF16)
    ki_o[...] = _layer_norm(k_idx, kig_ref[...], kib_ref[...]).astype(BF16)
    wit_o[...] = w_idx_t * (IDX_HEADS ** -0.5 * IDX_DIM ** -0.5)


def _dsa_proj(x2d, w_in, q_norm_g, kv_norm_g, w_uq, w_uk, w_qidx, kidx_g, kidx_b, ts=512):
    t, d = x2d.shape
    cw = DSA_ROWS
    o1, o2, o3 = Q_LORA, Q_LORA + KV_LORA, Q_LORA + KV_LORA + IDX_DIM
    w_in = w_in.astype(BF16)
    z = jnp.zeros((ATT_HEADS // 2, QK_HEAD, KV_LORA), F32)
    wuk_pair = jnp.concatenate([jnp.concatenate([w_uk[0::2], z], axis=2),
                                jnp.concatenate([z, w_uk[1::2]], axis=2)], axis=1).astype(BF16)
    wqi = w_qidx.reshape(Q_LORA, IDX_HEADS, IDX_DIM).transpose(1, 0, 2).astype(BF16)
    consts = [w_in[:, :o1], w_in[:, o1:o2], w_in[:, o2:o3], w_in[:, o3:].T, q_norm_g.reshape(1, -1),
              kv_norm_g.reshape(1, -1), w_uq.astype(BF16), wuk_pair, wqi, kidx_g.reshape(1, -1),
              kidx_b.reshape(1, -1)]
    out_shape = [jax.ShapeDtypeStruct((ATT_HEADS, t, KV_LORA), BF16),
                 jax.ShapeDtypeStruct((IDX_HEADS, t, IDX_DIM), BF16),
                 jax.ShapeDtypeStruct((t, IDX_DIM), BF16),
                 jax.ShapeDtypeStruct((IDX_HEADS, t), F32),
                 jax.ShapeDtypeStruct((t, KV_LORA), BF16),
                 jax.ShapeDtypeStruct((t // cw, KV_LORA, cw), BF16)]
    out_specs = [pl.BlockSpec((ATT_HEADS, ts, KV_LORA), lambda i: (0, i, 0)),
                 pl.BlockSpec((IDX_HEADS, ts, IDX_DIM), lambda i: (0, i, 0)),
                 pl.BlockSpec((ts, IDX_DIM), lambda i: (i, 0)),
                 pl.BlockSpec((IDX_HEADS, ts), lambda i: (0, i)),
                 pl.BlockSpec((ts, KV_LORA), lambda i: (i, 0)),
                 pl.BlockSpec((ts // cw, KV_LORA, cw), lambda i: (i, 0, 0))]
    return pl.pallas_call(
        functools.partial(_dsa_proj_kernel, cw=cw),
        grid=(t // ts,),
        in_specs=[pl.BlockSpec((ts, d), lambda i: (i, 0))] + [_const_spec(a.shape) for a in consts],
        out_specs=out_specs,
        out_shape=out_shape,
        compiler_params=pltpu.CompilerParams(dimension_semantics=("parallel",),
                                             vmem_limit_bytes=VMEM_LIMIT),
        name="dsa_proj",
    )(x2d, *consts)


def _bias_tiles_kernel(rb_ref, o_ref, *, qb):
    c = lax.broadcasted_iota(jnp.int32, (2 * qb, qb), 0)
    tl = lax.broadcasted_iota(jnp.int32, (2 * qb, qb), 1)
    n = jnp.maximum(qb + tl - c, 0)
    max_exact = REL_BUCKETS // 2
    nf = jnp.maximum(n, 1).astype(F32)
    large = max_exact + (jnp.log(nf / max_exact) / math.log(REL_MAX_DIST / max_exact)
                         * (REL_BUCKETS - max_exact)).astype(jnp.int32)
    large = jnp.minimum(large, REL_BUCKETS - 1)
    bucket = jnp.where(n < max_exact, n, large)
    for h in range(ATT_HEADS):
        acc = jnp.zeros((2 * qb, qb), F32)
        for bkt in range(REL_BUCKETS):
            acc = jnp.where(bucket == bkt, rb_ref[bkt, h], acc)
        o_ref[h] = (acc - rb_ref[REL_BUCKETS - 1, h]) * LOG2E


def _bias_tiles(rel_bias, qb):
    return pl.pallas_call(
        functools.partial(_bias_tiles_kernel, qb=qb),
        in_specs=[pl.BlockSpec(memory_space=pltpu.SMEM)],
        out_specs=pl.BlockSpec(memory_space=pltpu.VMEM),
        out_shape=jax.ShapeDtypeStruct((ATT_HEADS, 2 * qb, qb), F32),
        name="rel_bias_tiles",
    )(rel_bias)


def _dsa_attn_kernel(qi_ref, wit_ref, ki_ref, ckv_ref, ckvt_ref, qa_ref, near_ref, tril_ref, wuvt_ref,
                     wo_ref, x_ref, lng_ref, lnb_ref, o_ref, key_scr, khi_scr, klo_scr, madd_scr, m_scr, den_scr, acc_scr,
                     sc_scr, p_scr,
                     *, topk, qb, alpha):
    cw = qb
    i = pl.program_id(1)
    nch = i + 1
    t_pos = i * qb + lax.broadcasted_iota(jnp.int32, (1, qb), 1)
    s_loc = lax.broadcasted_iota(jnp.int32, (cw, 1), 0)
    int_min = jnp.int32(INT_MIN)

    qi_all = qi_ref[...].reshape(IDX_HEADS * qb, IDX_DIM)
    wit = wit_ref[...]

    def score_chunk(c, carry):
        off = pl.multiple_of(c * cw, cw)
        s_all = _bdot_nt(ki_ref[pl.ds(off, cw), :], qi_all)
        score = jnp.zeros((cw, qb), F32)
        for h in range(IDX_HEADS):
            score = score + jnp.maximum(s_all[:, h * qb:(h + 1) * qb], 0.0) * wit[h:h + 1]
        bits = lax.bitcast_convert_type(score, jnp.int32)
        key = jnp.where(bits < 0, bits ^ jnp.int32(0x7FFFFFFF), bits)
        key = jnp.where(off + s_loc <= t_pos, key, int_min)
        key_scr[c] = key
        khi_scr[c] = lax.shift_right_arithmetic(key, 16).astype(jnp.int16)
        klo_scr[c] = ((key & 0xFFFF) - 32768).astype(jnp.int16)
        return carry

    lax.fori_loop(0, nch, score_chunk, 0)

    def count(pred):
        def body(c, acc):
            kc = key_scr[c]
            for r in range(cw // SUBLANES):
                acc = acc + jnp.where(pred(kc[r * SUBLANES:(r + 1) * SUBLANES]), 1.0, 0.0)
            return acc

        acc = lax.fori_loop(0, nch, body, jnp.zeros((SUBLANES, qb), F32))
        return jnp.sum(acc, axis=0, keepdims=True)

    rows16 = 2 * SUBLANES
    one16, zero16, min16 = jnp.int16(1), jnp.int16(0), jnp.int16(-32768)

    def count16(scr, pred):
        def body(c, acc):
            kc = scr[c]
            for r in range(cw // rows16):
                acc = acc + jnp.where(pred(kc[r * rows16:(r + 1) * rows16]), one16, zero16)
            return acc

        acc = lax.fori_loop(0, nch, body, jnp.zeros((rows16, qb), jnp.int16))
        return jnp.sum(acc.astype(jnp.int32), axis=0, keepdims=True)

    def search16(scr, want):
        tau = jnp.where(count16(scr, lambda x: x >= zero16) >= want, jnp.int32(0), jnp.int32(-32768))

        def bit_step(bi, tau):
            cand = tau | lax.shift_left(jnp.int32(1), jnp.int32(14) - bi)
            c16 = cand.astype(jnp.int16)
            return jnp.where(count16(scr, lambda x: x >= c16) >= want, cand, tau)

        return lax.fori_loop(0, 15, bit_step, tau)

    tau_hi = search16(khi_scr, jnp.int32(topk))
    th16 = tau_hi.astype(jnp.int16)
    want_lo = topk - count16(khi_scr, lambda x: x > th16)

    def mask_chunk(c, carry):
        klo_scr[c] = jnp.where(khi_scr[c] == th16, klo_scr[c], min16)
        return carry

    lax.fori_loop(0, nch, mask_chunk, 0)
    tau_lo = search16(klo_scr, want_lo)
    tau = lax.shift_left(tau_hi, 16) | ((tau_lo + 32768) & 0xFFFF)
    kf = float(topk)

    need = kf - count(lambda kc: kc > tau)
    has_kth = tau > int_min
    tril = tril_ref[...]

    def select_chunk(c, run):
        kc = key_scr[c]
        eq = jnp.logical_and(kc == tau, has_kth)
        e = jnp.where(eq, 1.0, 0.0)
        rank = jnp.dot(tril, e.astype(BF16), preferred_element_type=F32) + run
        sel = jnp.logical_or(kc > tau, jnp.logical_and(eq, rank <= need))
        madd_scr[c] = jnp.where(sel, 0.0, NEG_MASK)
        return run + jnp.sum(e, axis=0, keepdims=True)

    lax.fori_loop(0, nch, select_chunk, jnp.zeros((1, qb), F32))

    m_scr[...] = jnp.full(m_scr.shape, -jnp.inf, F32)
    den_scr[...] = jnp.zeros(den_scr.shape, F32)
    acc_scr[...] = jnp.zeros(acc_scr.shape, F32)
    qa_all = qa_ref[...].reshape(ATT_HEADS * qb, KV_LORA)
    halves = qb // LANES

    def att_chunk(c, with_bias):
        off = pl.multiple_of(c * cw, cw)
        ckv_c = ckv_ref[pl.ds(off, cw), :]
        boff = pl.multiple_of(jnp.where(c == i, qb, 0), qb)
        logits = _bdot_nt(ckv_c, qa_all)
        for g in range(ATT_HEADS * halves):
            cols = slice(g * LANES, (g + 1) * LANES)
            qcols = slice((g % halves) * LANES, (g % halves + 1) * LANES)
            l = logits[:, cols] + madd_scr[c, :, qcols]
            if with_bias:
                l = l + near_ref[g // halves, pl.ds(boff, qb), qcols]
            m_old = m_scr[:, cols]
            m_new = jnp.maximum(m_old, jnp.max(l, axis=0, keepdims=True))
            p = jnp.exp2(l - m_new)
            scale = jnp.exp2(m_old - m_new)
            den_scr[:, cols] = den_scr[:, cols] * scale + jnp.sum(p, axis=0, keepdims=True)
            m_scr[:, cols] = m_new
            sc_scr[:, cols] = scale
            p_scr[:, cols] = p.astype(BF16)
        pv = jnp.dot(ckvt_ref[c], p_scr[...], preferred_element_type=F32)
        acc_scr[...] = acc_scr[...] * sc_scr[...] + pv

    def far_body(c, carry):
        att_chunk(c, False)
        return carry

    def near_body(c, carry):
        att_chunk(c, True)
        return carry

    c_near = jnp.maximum(i - 1, 0)
    lax.fori_loop(0, c_near, far_body, 0)
    lax.fori_loop(c_near, nch, near_body, 0)

    o_rows = []
    for h in range(ATT_HEADS):
        o_lat_t = acc_scr[:, h * qb:(h + 1) * qb] / den_scr[:, h * qb:(h + 1) * qb]
        o_rows.append(_bdot(wuvt_ref[h], o_lat_t))
    h_out = _bdot(jnp.concatenate(o_rows, axis=0).T, wo_ref[...])
    o_ref[...] = _layer_norm(alpha * x_ref[...] + h_out, lng_ref[...], lnb_ref[...])


def _dsa_attn(x2d, batch, seq, qa, qi, ki, wit, ckv, ckvt, near, w_uv, w_o, ln_g, ln_b, alpha):
    t, d = x2d.shape
    qb = DSA_ROWS
    assert KV_LORA == LANES
    nb = seq // qb
    topk = min(TOPK_MAX, seq // 4)
    tril = (jnp.arange(qb)[:, None] >= jnp.arange(qb)[None, :]).astype(BF16)
    wuv_t = w_uv.transpose(0, 2, 1).astype(BF16)
    wo = w_o.astype(BF16)
    tok = lambda b, i: (b * nb + i, 0)
    in_specs = [
        pl.BlockSpec((IDX_HEADS, qb, IDX_DIM), lambda b, i: (0, b * nb + i, 0)),
        pl.BlockSpec((IDX_HEADS, qb), lambda b, i: (0, b * nb + i)),
        pl.BlockSpec((seq, IDX_DIM), lambda b, i: (b, 0)),
        pl.BlockSpec((seq, KV_LORA), lambda b, i: (b, 0)),
        pl.BlockSpec((nb, KV_LORA, qb), lambda b, i: (b, 0, 0)),
        pl.BlockSpec((ATT_HEADS, qb, KV_LORA), lambda b, i: (0, b * nb + i, 0)),
        pl.BlockSpec(near.shape, lambda b, i: (0, 0, 0), pipeline_mode=pl.Buffered(1)),
        _const_spec(tril.shape),
        _const_spec(wuv_t.shape),
        _const_spec(wo.shape),
        pl.BlockSpec((qb, d), tok),
        _const_spec((1, d)),
        _const_spec((1, d)),
    ]
    scratch = [pltpu.VMEM((nb, qb, qb), jnp.int32),
               pltpu.VMEM((nb, qb, qb), jnp.int16),
               pltpu.VMEM((nb, qb, qb), jnp.int16),
               pltpu.VMEM((nb, qb, qb), F32),
               pltpu.VMEM((1, ATT_HEADS * qb), F32),
               pltpu.VMEM((1, ATT_HEADS * qb), F32),
               pltpu.VMEM((KV_LORA, ATT_HEADS * qb), F32),
               pltpu.VMEM((1, ATT_HEADS * qb), F32),
               pltpu.VMEM((qb, ATT_HEADS * qb), BF16)]
    return pl.pallas_call(
        functools.partial(_dsa_attn_kernel, topk=topk, qb=qb, alpha=alpha),
        grid=(batch, nb),
        in_specs=in_specs,
        out_specs=pl.BlockSpec((qb, d), tok),
        out_shape=jax.ShapeDtypeStruct((t, d), F32),
        scratch_shapes=scratch,
        compiler_params=pltpu.CompilerParams(dimension_semantics=("parallel", "parallel"),
                                             vmem_limit_bytes=VMEM_LIMIT),
        name="dsa_attn",
    )(qi, wit, ki, ckv, ckvt, qa, near, tril, wuv_t, wo, x2d, ln_g.reshape(1, d), ln_b.reshape(1, d))


def _from_scan_out_kernel(y_ref, g_ref, x_ref, wo_ref, lng_ref, lnb_ref, o_ref, zt_scr, w_scr, *, gb, d, n, alpha):
    heads = d // n
    tt = y_ref.shape[0]
    for t in range(tt):
        w_scr[pl.ds(t, n, stride=TIME_PITCH), :] = y_ref[t]
    for c in range(n):
        zt_scr[pl.ds(c, gb * heads, stride=PAIR_PITCH), :] = w_scr[pl.ds(c * TIME_PITCH, tt), :].T
    per_tile = LANES // n
    rows = []
    for b in range(gb):
        cols = []
        for j in range(d // LANES):
            xt = jnp.concatenate([zt_scr[pl.ds((b * heads + j * per_tile + hh) * PAIR_PITCH, n), :]
                                  for hh in range(per_tile)], axis=0)
            cols.append(xt.T)
        rows.append(jnp.concatenate(cols, axis=1))
    z = jnp.concatenate(rows, axis=0)
    h = _bdot(z * g_ref[...].reshape(gb * tt, d), wo_ref[...])
    out = _layer_norm(alpha * x_ref[...].reshape(gb * tt, d) + h, lng_ref[...], lnb_ref[...])
    o_ref[...] = out.reshape(gb, tt, d)


def _from_scan_out(y, g, x2d, w_o, ln_g, ln_b, alpha, gb):
    seq, n, _ = y.shape
    t, d = x2d.shape
    tt = LANES
    pairs = gb * d // n
    batch = t // seq
    blk = pl.BlockSpec((gb, tt, d), lambda gi, i: (gi, i, 0))
    consts = [w_o.astype(BF16), ln_g.reshape(1, d), ln_b.reshape(1, d)]
    return pl.pallas_call(
        functools.partial(_from_scan_out_kernel, gb=gb, d=d, n=n, alpha=alpha),
        grid=(batch // gb, seq // tt),
        in_specs=[pl.BlockSpec((tt, n, pairs), lambda gi, i: (i, 0, gi)), blk, blk]
                 + [_const_spec(a.shape) for a in consts],
        out_specs=blk,
        out_shape=jax.ShapeDtypeStruct((batch, seq, d), F32),
        scratch_shapes=[pltpu.VMEM((pairs * PAIR_PITCH, tt), F32), pltpu.VMEM((n * TIME_PITCH, pairs), F32)],
        compiler_params=pltpu.CompilerParams(dimension_semantics=("parallel", "parallel"),
                                             vmem_limit_bytes=VMEM_LIMIT),
        name="from_scan_out",
    )(y, g.reshape(batch, seq, d), x2d.reshape(batch, seq, d), *consts).reshape(t, d)


def kernel(x, ln_g, ln_b, rwkv_mix, rwkv_w_rkv, rwkv_w0, rwkv_w1, rwkv_w2, rwkv_a0, rwkv_a1, rwkv_a2, rwkv_v0, rwkv_v1, rwkv_v2, rwkv_g1, rwkv_g2, rwkv_k_k, rwkv_k_a, rwkv_r_k, rwkv_lnx_g, rwkv_lnx_b, rwkv_w_o, dsa_w_in, dsa_q_norm_g, dsa_kv_norm_g, dsa_w_uq, dsa_w_uk, dsa_w_uv, dsa_w_qidx, dsa_kidx_g, dsa_kidx_b, dsa_w_o, rel_bias, ffn_w_up, ffn_conv_w, ffn_conv_b, ffn_w_down):
    batch, seq, d = x.shape
    depth = ln_g.shape[0]
    heads = d // N_HEAD
    alpha = (2 * depth) ** 0.25
    gb = LANES // heads
    t = batch * seq
    x2d = x.reshape(t, d)

    def scan_param(v):
        return jnp.tile(v.reshape(heads, N_HEAD).T, (1, batch))

    near = _bias_tiles(rel_bias, DSA_ROWS)
    to_scan = functools.partial(_to_scan, gb=gb, seq=seq)

    v_first = None
    for i in range(depth):
        j = i // 2
        if i % 2 == 0:
            vres = None if j == 0 else (v_first, rwkv_v0[j - 1], rwkv_v1[j - 1], rwkv_v2[j - 1])
            r, w, k, v, lr, gate = _rwkv_proj(
                x2d, seq, rwkv_mix[j], rwkv_w_rkv[j], rwkv_w0[j], rwkv_w1[j], rwkv_w2[j], rwkv_a0[j],
                rwkv_a1[j], rwkv_a2[j], rwkv_g1[j], rwkv_g2[j], vres)
            if j == 0:
                v_first = v
            params = [scan_param(q) for q in (rwkv_k_k[j], rwkv_k_a[j], rwkv_r_k[j].reshape(-1),
                                              rwkv_lnx_g[j], rwkv_lnx_b[j])]
            z = _wkv(to_scan(r), to_scan(w), to_scan(k), to_scan(v), to_scan(lr), params)
            x2d = _from_scan_out(z, gate, x2d, rwkv_w_o[j], ln_g[i, 0], ln_b[i, 0], alpha, gb)
        else:
            qa, qi, ki, wit, ckv, ckvt = _dsa_proj(x2d, dsa_w_in[j], dsa_q_norm_g[j], dsa_kv_norm_g[j],
                                                   dsa_w_uq[j], dsa_w_uk[j], dsa_w_qidx[j], dsa_kidx_g[j],
                                                   dsa_kidx_b[j])
            x2d = _dsa_attn(x2d, batch, seq, qa, qi, ki, wit, ckv, ckvt, near, dsa_w_uv[j], dsa_w_o[j],
                            ln_g[i, 0], ln_b[i, 0], alpha)
        x2d = _ffn(x2d, seq, ffn_w_up[i], ffn_conv_w[i], ffn_conv_b[i], ffn_w_down[i],
                   ln_g[i, 1], ln_b[i, 1], alpha)
    return x2d.reshape(batch, seq, d)
```
